```python
import jax, jax.numpy as jnp
from jax import lax
import numpy as np

D_MODEL = 2048
BATCH = 4
SEQ = 4096
DEPTH = 1

HEAD_DIM = 128
D_ATTN = D_MODEL // 2
N_ATTN_HEADS = D_ATTN // HEAD_DIM
D_GMLP = D_MODEL - D_ATTN
N_GMLP_HEADS = D_GMLP // HEAD_DIM
D_MIX = D_ATTN + D_GMLP
CHUNK = 128
Q_BLOCK = 128
D_FF = 4 * D_MODEL
D_IN_PROJ = 3 * D_ATTN + N_ATTN_HEADS + 2 * D_GMLP
EPS = 1e-6

kernel_name = "hymba_fox_gmlp_hybrid_block"


def rmsnorm(x, g):
    xf = x.astype(jnp.float32)
    y = xf * lax.rsqrt(jnp.mean(xf * xf, axis=-1, keepdims=True) + EPS)
    return (y * g.astype(jnp.float32)).astype(x.dtype)


def layernorm(x, g, b):
    xf = x.astype(jnp.float32)
    mu = jnp.mean(xf, axis=-1, keepdims=True)
    xc = xf - mu
    y = xc * lax.rsqrt(jnp.mean(xc * xc, axis=-1, keepdims=True) + EPS)
    return (y * g.astype(jnp.float32) + b.astype(jnp.float32)).astype(x.dtype)


def forgetting_attention(q, k, v, log_f):
    B, S, H, D = q.shape
    nb = S // Q_BLOCK
    scale = 1.0 / np.sqrt(D).astype(np.float32)
    F = jnp.cumsum(log_f, axis=1).transpose(0, 2, 1)
    q_blocks = q.reshape(B, nb, Q_BLOCK, H, D).transpose(1, 0, 2, 3, 4)
    F_blocks = F.reshape(B, H, nb, Q_BLOCK).transpose(2, 0, 1, 3)
    k_pos = jnp.arange(S)

    def one_block(args):
        qi, Fq, i = args
        s = jnp.einsum('bqhd,bkhd->bhqk', qi, k, preferred_element_type=jnp.float32) * scale
        s = s + Fq[..., :, None] - F[:, :, None, :]
        q_pos = i * Q_BLOCK + jnp.arange(Q_BLOCK)
        causal = k_pos[None, :] <= q_pos[:, None]
        s = jnp.where(causal[None, None], s, -jnp.inf)
        p = jax.nn.softmax(s, axis=-1)
        return jnp.einsum('bhqk,bkhd->bqhd', p.astype(v.dtype), v)

    out = lax.map(one_block, (q_blocks, F_blocks, jnp.arange(nb)))
    return out.transpose(1, 0, 2, 3, 4).reshape(B, S, H * D)


def chunked_spatial_gating(zu, zv, ln_g, ln_b, w_s, b_s):
    B, S, _ = zu.shape
    nc = S // CHUNK
    u = jax.nn.gelu(zu)
    v = layernorm(jax.nn.gelu(zv), ln_g, ln_b)
    v = v.reshape(B, nc, CHUNK, N_GMLP_HEADS, HEAD_DIM)
    w_causal = jnp.tril(w_s)
    mix = jnp.einsum('hts,bcshd->bcthd', w_causal.astype(v.dtype), v)
    mix = mix + b_s.T[None, None, :, :, None]
    out = u.reshape(B, nc, CHUNK, N_GMLP_HEADS, HEAD_DIM) * mix
    return out.reshape(B, S, D_GMLP)


def setup_inputs(seed: int = 0) -> dict:
    key = jax.random.key(seed)
    ks = jax.random.split(key, 20)
    L = DEPTH
    nrm = jax.random.normal
    x = nrm(ks[0], (BATCH, SEQ, D_MODEL), jnp.float32)
    norm_mix_g = 1.0 + 0.02 * nrm(ks[1], (L, D_MODEL), jnp.float32)
    w_qkv = nrm(ks[2], (L, D_MODEL, 3 * D_ATTN), jnp.float32) * D_MODEL ** -0.5
    w_f = nrm(ks[3], (L, D_MODEL, N_ATTN_HEADS), jnp.float32) * 0.1 * D_MODEL ** -0.5
    w_g = nrm(ks[4], (L, D_MODEL, 2 * D_GMLP), jnp.float32) * D_MODEL ** -0.5
    w_in = jnp.concatenate([w_qkv, w_f, w_g], axis=-1)
    b_f = jax.random.uniform(ks[5], (L, N_ATTN_HEADS), jnp.float32, 1.0, 5.0)
    gmlp_ln_g = 1.0 + 0.02 * nrm(ks[6], (L, D_GMLP), jnp.float32)
    gmlp_ln_b = 0.02 * nrm(ks[7], (L, D_GMLP), jnp.float32)
    w_s = nrm(ks[8], (L, N_GMLP_HEADS, CHUNK, CHUNK), jnp.float32) * CHUNK ** -0.5
    b_s = 1.0 + 0.1 * nrm(ks[9], (L, N_GMLP_HEADS, CHUNK), jnp.float32)
    attn_out_g = 1.0 + 0.02 * nrm(ks[10], (L, D_ATTN), jnp.float32)
    gmlp_out_g = 1.0 + 0.02 * nrm(ks[11], (L, D_GMLP), jnp.float32)
    w_out = nrm(ks[12], (L, D_MIX, D_MODEL), jnp.float32) * D_MIX ** -0.5
    norm_ffn_g = 1.0 + 0.02 * nrm(ks[13], (L, D_MODEL), jnp.float32)
    w_ff1 = nrm(ks[14], (L, D_MODEL, D_FF), jnp.float32) * D_MODEL ** -0.5
    w_ff2 = nrm(ks[15], (L, D_FF, D_MODEL), jnp.float32) * D_FF ** -0.5
    norm_final_g = 1.0 + 0.02 * nrm(ks[16], (D_MODEL,), jnp.float32)
    return {"x": x, "norm_mix_g": norm_mix_g, "w_in": w_in, "b_f": b_f,
            "gmlp_ln_g": gmlp_ln_g, "gmlp_ln_b": gmlp_ln_b, "w_s": w_s, "b_s": b_s,
            "attn_out_g": attn_out_g, "gmlp_out_g": gmlp_out_g, "w_out": w_out,
            "norm_ffn_g": norm_ffn_g, "w_ff1": w_ff1, "w_ff2": w_ff2,
            "norm_final_g": norm_final_g}


def reference(x, norm_mix_g, w_in, b_f, gmlp_ln_g, gmlp_ln_b, w_s, b_s,
              attn_out_g, gmlp_out_g, w_out, norm_ffn_g, w_ff1, w_ff2, norm_final_g):
    B, S, _ = x.shape
    for l in range(DEPTH):
        h = rmsnorm(x, norm_mix_g[l])
        z = jnp.einsum('bsd,de->bse', h, w_in[l])
        o1 = D_ATTN; o2 = 2 * D_ATTN; o3 = 3 * D_ATTN; o4 = o3 + N_ATTN_HEADS
        q = z[..., :o1].reshape(B, S, N_ATTN_HEADS, HEAD_DIM)
        k = z[..., o1:o2].reshape(B, S, N_ATTN_HEADS, HEAD_DIM)
        v = z[..., o2:o3].reshape(B, S, N_ATTN_HEADS, HEAD_DIM)
        log_f = jax.nn.log_sigmoid(z[..., o3:o4].astype(jnp.float32) + b_f[l].astype(jnp.float32))
        zu = z[..., o4:o4 + D_GMLP]
        zv = z[..., o4 + D_GMLP:]
        attn = forgetting_attention(q, k, v, log_f)
        gm = chunked_spatial_gating(zu, zv, gmlp_ln_g[l], gmlp_ln_b[l], w_s[l], b_s[l])
        merged = jnp.concatenate([rmsnorm(attn, attn_out_g[l]),
                                  rmsnorm(gm, gmlp_out_g[l])], axis=-1)
        x = x + jnp.einsum('bse,ed->bsd', merged, w_out[l])
        h2 = rmsnorm(x, norm_ffn_g[l])
        a = jax.nn.relu(jnp.einsum('bsd,df->bsf', h2, w_ff1[l]))
        x = x + jnp.einsum('bsf,fd->bsd', a * a, w_ff2[l])
    return rmsnorm(x, norm_final_g)
```

```python
import functools
import math

import numpy as np
import jax
import jax.numpy as jnp
from jax import lax
from jax.experimental import pallas as pl
from jax.experimental.pallas import tpu as pltpu

F32 = jnp.float32
BF16 = jnp.bfloat16

HEAD_DIM = 128
CHUNK = 128
EPS = 1e-6
LANES = 128
LOG2E = 1.4426950408889634
VMEM_LIMIT = 56 * 1024 * 1024

TM_PROJ = 1024
TN_PROJ = 512
TQ = 512
TK = 512
TM_GMLP = 512
TM_OUT = 512
TM_FFN = 1024
TF_FFN = 512
ROWS = 128
CUM_BLK = 256


def _rms(x, g):
    ms = jnp.mean(x * x, axis=-1, keepdims=True)
    return x * lax.rsqrt(ms + EPS) * g


def _gelu(x):
    c = math.sqrt(2.0 / math.pi)
    return x * (0.5 * (1.0 + jnp.tanh(c * (x + 0.044715 * (x * x * x)))))


def _split3(x):
    hi = x.astype(BF16)
    r1 = x - hi.astype(F32)
    mid = r1.astype(BF16)
    lo = (r1 - mid.astype(F32)).astype(BF16)
    return hi, mid, lo


def _params(*sem):
    return pltpu.CompilerParams(dimension_semantics=sem, vmem_limit_bytes=VMEM_LIMIT)


def _inproj_kernel(x_ref, g_ref, w_ref, wf_ref, z_ref, zf_ref, h_ref, *, n_q_tiles, q_scale):
    j = pl.program_id(1)
    tm = x_ref.shape[0]

    @pl.when(j == 0)
    def _():
        def body(r, c):
            sl = pl.ds(pl.multiple_of(r * ROWS, ROWS), ROWS)
            h_ref[sl, :] = _rms(x_ref[sl, :], g_ref[...]).astype(BF16)
            return c
        lax.fori_loop(0, tm // ROWS, body, 0)
        zf_ref[...] = jnp.dot(h_ref[...], wf_ref[...], preferred_element_type=F32)

    acc = jnp.dot(h_ref[...], w_ref[...], preferred_element_type=F32)
    scale = jnp.where(j < n_q_tiles, q_scale, 1.0).astype(F32)
    z_ref[...] = (acc * scale).astype(BF16)


def _inproj(x2d, g, w_main, w_f, *, d_attn):
    t, d = x2d.shape
    n = w_main.shape[1]
    q_scale = LOG2E / math.sqrt(HEAD_DIM)
    return pl.pallas_call(
        functools.partial(_inproj_kernel, n_q_tiles=d_attn // TN_PROJ, q_scale=q_scale),
        grid=(t // TM_PROJ, n // TN_PROJ),
        in_specs=[
            pl.BlockSpec((TM_PROJ, d), lambda i, j: (i, 0)),
            pl.BlockSpec((1, d), lambda i, j: (0, 0)),
            pl.BlockSpec((d, TN_PROJ), lambda i, j: (0, j)),
            pl.BlockSpec((d, LANES), lambda i, j: (0, 0)),
        ],
        out_specs=[
            pl.BlockSpec((TM_PROJ, TN_PROJ), lambda i, j: (i, j)),
            pl.BlockSpec((TM_PROJ, LANES), lambda i, j: (i, 0)),
        ],
        out_shape=[
            jax.ShapeDtypeStruct((t, n), BF16),
            jax.ShapeDtypeStruct((t, LANES), F32),
        ],
        scratch_shapes=[pltpu.VMEM((TM_PROJ, d), BF16)],
        compiler_params=_params("parallel", "arbitrary"),
        name="inproj",
    )(x2d, g, w_main, w_f)


def _decay_kernel(zf_ref, bf_ref, selq_ref, selk_ref, qa_ref, ka_ref, f_ref):
    s = zf_ref.shape[0]
    row = lax.broadcasted_iota(jnp.int32, (CUM_BLK, CUM_BLK), 0)
    col = lax.broadcasted_iota(jnp.int32, (CUM_BLK, CUM_BLK), 1)
    tri = (col <= row).astype(BF16)
    carry = jnp.zeros((1, LANES), F32)
    for r in range(s // CUM_BLK):
        sl = slice(r * CUM_BLK, (r + 1) * CUM_BLK)
        x = zf_ref[sl, :] + bf_ref[...]
        logf = jnp.minimum(x, 0.0) - jnp.log1p(jnp.exp(-jnp.abs(x)))
        hi, mid, lo = _split3(logf)
        c = (jnp.dot(tri, hi, preferred_element_type=F32)
             + jnp.dot(tri, mid, preferred_element_type=F32)
             + jnp.dot(tri, lo, preferred_element_type=F32)) + carry
        f_ref[sl, :] = c
        carry = c[CUM_BLK - 1:CUM_BLK, :]

    blk = 512
    ones = jnp.ones((blk, LANES), BF16)

    def body(r, c):
        sl = pl.ds(pl.multiple_of(r * blk, blk), blk)
        hi, mid, lo = _split3(f_ref[sl, :] * LOG2E)
        lhs = jnp.concatenate([hi, mid, lo, ones], axis=1)
        qa_ref[sl, :] = jnp.dot(lhs, selq_ref[...], preferred_element_type=F32).astype(BF16)
        ka_ref[sl, :] = jnp.dot(lhs, selk_ref[...], preferred_element_type=F32).astype(BF16)
        return c
    lax.fori_loop(0, s // blk, body, 0)


def _decay_selectors(n_heads):
    selq = np.zeros((4 * LANES, n_heads * LANES), np.float32)
    selk = np.zeros((4 * LANES, n_heads * LANES), np.float32)
    for h in range(n_heads):
        for p in range(3):
            selq[p * LANES + h, h * LANES + p] = 1.0
            selq[3 * LANES, h * LANES + 3 + p] = 1.0
            selk[3 * LANES, h * LANES + p] = 1.0
            selk[p * LANES + h, h * LANES + 3 + p] = -1.0
    return jnp.asarray(selq, BF16), jnp.asarray(selk, BF16)


def _decay(zf, bf_row, *, batch, seq, n_heads):
    selq, selk = _decay_selectors(n_heads)
    w = n_heads * LANES
    return pl.pallas_call(
        _decay_kernel,
        grid=(batch,),
        in_specs=[
            pl.BlockSpec((seq, LANES), lambda b: (b, 0)),
            pl.BlockSpec((1, LANES), lambda b: (0, 0)),
            pl.BlockSpec((4 * LANES, w), lambda b: (0, 0)),
            pl.BlockSpec((4 * LANES, w), lambda b: (0, 0)),
        ],
        out_specs=[
            pl.BlockSpec((seq, w), lambda b: (b, 0)),
            pl.BlockSpec((seq, w), lambda b: (b, 0)),
        ],
        out_shape=[
            jax.ShapeDtypeStruct((batch * seq, w), BF16),
            jax.ShapeDtypeStruct((batch * seq, w), BF16),
        ],
        scratch_shapes=[pltpu.VMEM((seq, LANES), F32)],
        compiler_params=_params("parallel"),
        name="decay",
    )(zf, bf_row, selq, selk)


def _attn_kernel(q_ref, qa_ref, k_ref, ka_ref, v_ref, o_ref, m_ref, acc_ref):
    qi = pl.program_id(2)
    tq = q_ref.shape[0]
    qp = jnp.concatenate([q_ref[...], qa_ref[...]], axis=1)
    m_ref[...] = jnp.full(m_ref.shape, -jnp.inf, F32)
    acc_ref[...] = jnp.zeros(acc_ref.shape, F32)
    ones = jnp.ones((TK, HEAD_DIM), BF16)

    def step(j, masked):
        rows = pl.ds(pl.multiple_of(j * TK, TK), TK)
        kp = jnp.concatenate([k_ref[rows, :], ka_ref[rows, :]], axis=1)
        s = lax.dot_general(qp, kp, (((1,), (1,)), ((), ())), preferred_element_type=F32)
        if masked:
            r = lax.broadcasted_iota(jnp.int32, (tq, TK), 0)
            c = lax.broadcasted_iota(jnp.int32, (tq, TK), 1)
            s = jnp.where(c <= r, s, -jnp.inf)
        m_prev = m_ref[...]
        m_new = jnp.maximum(m_prev, jnp.max(s, axis=1, keepdims=True))
        alpha = jnp.exp2(m_prev - m_new)
        p = jnp.exp2(s - jnp.concatenate([m_new] * (TK // LANES), axis=1))
        vp = jnp.concatenate([v_ref[rows, :], ones], axis=1)
        pv = jnp.dot(p.astype(BF16), vp, preferred_element_type=F32)
        acc_ref[...] = acc_ref[...] * jnp.concatenate([alpha, alpha], axis=1) + pv
        m_ref[...] = m_new

    def loop_body(j, c):
        step(j, False)
        return c
    lax.fori_loop(0, qi, loop_body, 0)
    step(qi, True)
    acc = acc_ref[...]
    o_ref[...] = (acc[:, :HEAD_DIM] / acc[:, HEAD_DIM:]).astype(BF16)


def _attention(z, qa, ka, *, batch, seq, n_heads):
    assert TQ == TK
    nq = seq // TQ
    t = batch * seq
    return pl.pallas_call(
        _attn_kernel,
        grid=(batch, n_heads, nq),
        in_specs=[
            pl.BlockSpec((TQ, HEAD_DIM), lambda b, h, i: (b * nq + i, h)),
            pl.BlockSpec((TQ, LANES), lambda b, h, i: (b * nq + i, h)),
            pl.BlockSpec((seq, HEAD_DIM), lambda b, h, i: (b, n_heads + h)),
            pl.BlockSpec((seq, LANES), lambda b, h, i: (b, h)),
            pl.BlockSpec((seq, HEAD_DIM), lambda b, h, i: (b, 2 * n_heads + h)),
        ],
        out_specs=pl.BlockSpec((TQ, HEAD_DIM), lambda b, h, i: (b * nq + i, h)),
        out_shape=jax.ShapeDtypeStruct((t, n_heads * HEAD_DIM), BF16),
        scratch_shapes=[pltpu.VMEM((TQ, LANES), F32), pltpu.VMEM((TQ, 2 * HEAD_DIM), F32)],
        compiler_params=_params("parallel", "parallel", "arbitrary"),
        name="attention",
    )(z, qa, z, ka, z)


def _gmlp_kernel(zu_ref, zv_ref, lng_ref, lnb_ref, ws_ref, bs_ref, og_ref, o_ref,
                 w_sc, v_sc, gm_sc):
    tm = zu_ref.shape[0]
    n_heads = ws_ref.shape[0]

    @pl.when(pl.program_id(0) == 0)
    def _():
        r = lax.broadcasted_iota(jnp.int32, (CHUNK, CHUNK), 0)
        c = lax.broadcasted_iota(jnp.int32, (CHUNK, CHUNK), 1)
        for h in range(n_heads):
            w_sc[h] = jnp.where(c <= r, ws_ref[h], 0.0).astype(BF16)

    for cc in range(tm // CHUNK):
        rows = slice(cc * CHUNK, (cc + 1) * CHUNK)
        gv = _gelu(zv_ref[rows, :].astype(F32))
        mu = jnp.mean(gv, axis=-1, keepdims=True)
        xc = gv - mu
        var = jnp.mean(xc * xc, axis=-1, keepdims=True)
        v_sc[rows, :] = (xc * lax.rsqrt(var + EPS) * lng_ref[...] + lnb_ref[...]).astype(BF16)

    for cc in range(tm // CHUNK):
        rows = slice(cc * CHUNK, (cc + 1) * CHUNK)
        for h in range(n_heads):
            cols = slice(h * HEAD_DIM, (h + 1) * HEAD_DIM)
            mix = jnp.dot(w_sc[h], v_sc[rows, cols], preferred_element_type=F32) + bs_ref[h]
            gm_sc[rows, cols] = _gelu(zu_ref[rows, cols].astype(F32)) * mix

    for cc in range(tm // CHUNK):
        rows = slice(cc * CHUNK, (cc + 1) * CHUNK)
        o_ref[rows, :] = _rms(gm_sc[rows, :], og_ref[...]).astype(BF16)


def _gmlp(z, ln_g, ln_b, w_s, b_s3, out_g, *, d_attn, d_gmlp):
    t = z.shape[0]
    n_heads = w_s.shape[0]
    ucol = 3 * d_attn // d_gmlp
    return pl.pallas_call(
        _gmlp_kernel,
        grid=(t // TM_GMLP,),
        in_specs=[
            pl.BlockSpec((TM_GMLP, d_gmlp), lambda i: (i, ucol)),
            pl.BlockSpec((TM_GMLP, d_gmlp), lambda i: (i, ucol + 1)),
            pl.BlockSpec((1, d_gmlp), lambda i: (0, 0)),
            pl.BlockSpec((1, d_gmlp), lambda i: (0, 0)),
            pl.BlockSpec((n_heads, CHUNK, CHUNK), lambda i: (0, 0, 0)),
            pl.BlockSpec((n_heads, CHUNK, 1), lambda i: (0, 0, 0)),
            pl.BlockSpec((1, d_gmlp), lambda i: (0, 0)),
        ],
        out_specs=pl.BlockSpec((TM_GMLP, d_gmlp), lambda i: (i, 0)),
        out_shape=jax.ShapeDtypeStruct((t, d_gmlp), BF16),
        scratch_shapes=[
            pltpu.VMEM((n_heads, CHUNK, CHUNK), BF16),
            pltpu.VMEM((TM_GMLP, d_gmlp), BF16),
            pltpu.VMEM((TM_GMLP, d_gmlp), F32),
        ],
        compiler_params=_params("arbitrary"),
        name="gmlp",
    )(z, z, ln_g, ln_b, w_s, b_s3, out_g)


def _outproj_kernel(a_ref, gm_ref, x_ref, ag_ref, w_ref, o_ref, an_ref):
    tm = a_ref.shape[0]
    d_attn = a_ref.shape[1]

    def body(r, c):
        sl = pl.ds(pl.multiple_of(r * ROWS, ROWS), ROWS)
        an_ref[sl, :] = _rms(a_ref[sl, :].astype(F32), ag_ref[...]).astype(BF16)
        return c
    lax.fori_loop(0, tm // ROWS, body, 0)
    y = (jnp.dot(an_ref[...], w_ref[:d_attn, :], preferred_element_type=F32)
         + jnp.dot(gm_ref[...], w_ref[d_attn:, :], preferred_element_type=F32))
    o_ref[...] = x_ref[...] + y


def _outproj(attn, gmn, x2d, attn_g, w_out):
    t, d = x2d.shape
    d_attn = attn.shape[1]
    d_gmlp = gmn.shape[1]
    return pl.pallas_call(
        _outproj_kernel,
        grid=(t // TM_OUT,),
        in_specs=[
            pl.BlockSpec((TM_OUT, d_attn), lambda i: (i, 0)),
            pl.BlockSpec((TM_OUT, d_gmlp), lambda i: (i, 0)),
            pl.BlockSpec((TM_OUT, d), lambda i: (i, 0)),
            pl.BlockSpec((1, d_attn), lambda i: (0, 0)),
            pl.BlockSpec((d_attn + d_gmlp, d), lambda i: (0, 0)),
        ],
        out_specs=pl.BlockSpec((TM_OUT, d), lambda i: (i, 0)),
        out_shape=jax.ShapeDtypeStruct((t, d), F32),
        scratch_shapes=[pltpu.VMEM((TM_OUT, d_attn), BF16)],
        compiler_params=_params("parallel"),
        name="outproj",
    )(attn, gmn, x2d, attn_g, w_out)


def _ffn_kernel(x_ref, g_ref, w1_ref, w2_ref, gf_ref, o_ref, h_ref, *, final_norm):
    f = pl.program_id(1)
    tm = x_ref.shape[0]

    @pl.when(f == 0)
    def _():
        def body(r, c):
            sl = pl.ds(pl.multiple_of(r * ROWS, ROWS), ROWS)
            h_ref[sl, :] = _rms(x_ref[sl, :], g_ref[...]).astype(BF16)
            o_ref[sl, :] = jnp.zeros((ROWS, o_ref.shape[1]), F32)
            return c
        lax.fori_loop(0, tm // ROWS, body, 0)

    a = jnp.maximum(jnp.dot(h_ref[...], w1_ref[...], preferred_element_type=F32), 0.0)
    o_ref[...] += jnp.dot((a * a).astype(BF16), w2_ref[...], preferred_element_type=F32)

    @pl.when(f == pl.num_programs(1) - 1)
    def _():
        def body(r, c):
            sl = pl.ds(pl.multiple_of(r * ROWS, ROWS), ROWS)
            y = x_ref[sl, :] + o_ref[sl, :]
            if final_norm:
                y = _rms(y, gf_ref[...])
            o_ref[sl, :] = y
            return c
        lax.fori_loop(0, tm // ROWS, body, 0)


def _ffn(x2d, g, w1, w2, g_final, *, final_norm):
    t, d = x2d.shape
    d_ff = w1.shape[1]
    return pl.pallas_call(
        functools.partial(_ffn_kernel, final_norm=final_norm),
        grid=(t // TM_FFN, d_ff // TF_FFN),
        in_specs=[
            pl.BlockSpec((TM_FFN, d), lambda i, f: (i, 0)),
            pl.BlockSpec((1, d), lambda i, f: (0, 0)),
            pl.BlockSpec((d, TF_FFN), lambda i, f: (0, f)),
            pl.BlockSpec((TF_FFN, d), lambda i, f: (f, 0)),
            pl.BlockSpec((1, d), lambda i, f: (0, 0)),
        ],
        out_specs=pl.BlockSpec((TM_FFN, d), lambda i, f: (i, 0)),
        out_shape=jax.ShapeDtypeStruct((t, d), F32),
        scratch_shapes=[pltpu.VMEM((TM_FFN, d), BF16)],
        compiler_params=_params("parallel", "arbitrary"),
        name="ffn",
    )(x2d, g, w1, w2, g_final)


def kernel(x, norm_mix_g, w_in, b_f, gmlp_ln_g, gmlp_ln_b, w_s, b_s, attn_out_g, gmlp_out_g,
           w_out, norm_ffn_g, w_ff1, w_ff2, norm_final_g):
    batch, seq, d = x.shape
    depth = w_in.shape[0]
    n_heads = b_f.shape[1]
    d_attn = n_heads * HEAD_DIM
    d_gmlp = gmlp_ln_g.shape[1]
    o3 = 3 * d_attn
    o4 = o3 + n_heads
    assert d_attn == d_gmlp and w_in.shape[2] == o4 + 2 * d_gmlp

    x2d = x.reshape(batch * seq, d)
    for l in range(depth):
        w_main = jnp.concatenate([w_in[l][:, :o3], w_in[l][:, o4:]], axis=1).astype(BF16)
        w_f = jnp.pad(w_in[l][:, o3:o4], ((0, 0), (0, LANES - n_heads))).astype(BF16)
        bf_row = jnp.pad(b_f[l].astype(F32), (0, LANES - n_heads)).reshape(1, LANES)

        z, zf = _inproj(x2d, norm_mix_g[l].reshape(1, d), w_main, w_f, d_attn=d_attn)
        qa, ka = _decay(zf, bf_row, batch=batch, seq=seq, n_heads=n_heads)
        attn = _attention(z, qa, ka, batch=batch, seq=seq, n_heads=n_heads)
        gmn = _gmlp(z, gmlp_ln_g[l].reshape(1, d_gmlp), gmlp_ln_b[l].reshape(1, d_gmlp),
                    w_s[l], b_s[l].reshape(n_heads, CHUNK, 1), gmlp_out_g[l].reshape(1, d_gmlp),
                    d_attn=d_attn, d_gmlp=d_gmlp)
        x2d = _outproj(attn, gmn, x2d, attn_out_g[l].reshape(1, d_attn), w_out[l].astype(BF16))
        x2d = _ffn(x2d, norm_ffn_g[l].reshape(1, d), w_ff1[l].astype(BF16), w_ff2[l].astype(BF16),
                   norm_final_g.reshape(1, d), final_norm=(l == depth - 1))
    return x2d.reshape(batch, seq, d)
```

```python
import functools
import math

import numpy as np
import jax
import jax.numpy as jnp
from jax import lax
from jax.experimental import pallas as pl
from jax.experimental.pallas import tpu as pltpu

F32 = jnp.float32
BF16 = jnp.bfloat16

HEAD_DIM = 128
CHUNK = 128
EPS = 1e-6
LANES = 128
LOG2E = 1.4426950408889634
VMEM_LIMIT = 56 * 1024 * 1024

TM_PROJ = 1024
TN_PROJ = 1024
TQ = 512
TK = 512
ATTN_UNROLL = 4
TM_GMLP = 512
TM_OUT = 512
TM_FFN = 1024
TF_FFN = 512
ROWS = 128
CUM_BLK = 256


def _rms(x, g):
    ms = jnp.mean(x * x, axis=-1, keepdims=True)
    return x * lax.rsqrt(ms + EPS) * g


def _gelu(x):
    c = math.sqrt(2.0 / math.pi)
    return x * (0.5 * (1.0 + jnp.tanh(c * (x + 0.044715 * (x * x * x)))))


def _split3(x):
    hi = x.astype(BF16)
    r1 = x - hi.astype(F32)
    mid = r1.astype(BF16)
    lo = (r1 - mid.astype(F32)).astype(BF16)
    return hi, mid, lo


def _params(*sem):
    return pltpu.CompilerParams(dimension_semantics=sem, vmem_limit_bytes=VMEM_LIMIT)


def _inproj_kernel(x_ref, g_ref, w_ref, wf_ref, z_ref, zf_ref, h_ref, *, n_q_tiles, q_scale):
    j = pl.program_id(1)
    tm = x_ref.shape[0]

    @pl.when(j == 0)
    def _():
        def body(r, c):
            sl = pl.ds(pl.multiple_of(r * ROWS, ROWS), ROWS)
            h_ref[sl, :] = _rms(x_ref[sl, :], g_ref[...]).astype(BF16)
            return c
        lax.fori_loop(0, tm // ROWS, body, 0)
        zf_ref[...] = jnp.dot(h_ref[...], wf_ref[...], preferred_element_type=F32)

    acc = jnp.dot(h_ref[...], w_ref[...], preferred_element_type=F32)
    scale = jnp.where(j < n_q_tiles, q_scale, 1.0).astype(F32)
    z_ref[...] = (acc * scale).astype(BF16)


def _inproj(x2d, g, w_main, w_f, *, d_attn):
    t, d = x2d.shape
    n = w_main.shape[1]
    q_scale = LOG2E / math.sqrt(HEAD_DIM)
    return pl.pallas_call(
        functools.partial(_inproj_kernel, n_q_tiles=d_attn // TN_PROJ, q_scale=q_scale),
        grid=(t // TM_PROJ, n // TN_PROJ),
        in_specs=[
            pl.BlockSpec((TM_PROJ, d), lambda i, j: (i, 0)),
            pl.BlockSpec((1, d), lambda i, j: (0, 0)),
            pl.BlockSpec((d, TN_PROJ), lambda i, j: (0, j)),
            pl.BlockSpec((d, LANES), lambda i, j: (0, 0)),
        ],
        out_specs=[
            pl.BlockSpec((TM_PROJ, TN_PROJ), lambda i, j: (i, j)),
            pl.BlockSpec((TM_PROJ, LANES), lambda i, j: (i, 0)),
        ],
        out_shape=[
            jax.ShapeDtypeStruct((t, n), BF16),
            jax.ShapeDtypeStruct((t, LANES), F32),
        ],
        scratch_shapes=[pltpu.VMEM((TM_PROJ, d), BF16)],
        compiler_params=_params("parallel", "arbitrary"),
        name="inproj",
    )(x2d, g, w_main, w_f)


def _decay_kernel(zf_ref, bf_ref, selq_ref, selk_ref, qa_ref, ka_ref, f_ref):
    s = zf_ref.shape[0]
    row = lax.broadcasted_iota(jnp.int32, (CUM_BLK, CUM_BLK), 0)
    col = lax.broadcasted_iota(jnp.int32, (CUM_BLK, CUM_BLK), 1)
    tri = (col <= row).astype(BF16)
    carry = jnp.zeros((1, LANES), F32)
    for r in range(s // CUM_BLK):
        sl = slice(r * CUM_BLK, (r + 1) * CUM_BLK)
        x = zf_ref[sl, :] + bf_ref[...]
        logf = jnp.minimum(x, 0.0) - jnp.log1p(jnp.exp(-jnp.abs(x)))
        hi, mid, lo = _split3(logf)
        c = (jnp.dot(tri, hi, preferred_element_type=F32)
             + jnp.dot(tri, mid, preferred_element_type=F32)
             + jnp.dot(tri, lo, preferred_element_type=F32)) + carry
        f_ref[sl, :] = c
        carry = c[CUM_BLK - 1:CUM_BLK, :]

    blk = 512
    ones = jnp.ones((blk, LANES), BF16)

    def body(r, c):
        sl = pl.ds(pl.multiple_of(r * blk, blk), blk)
        hi, mid, lo = _split3(f_ref[sl, :] * LOG2E)
        lhs = jnp.concatenate([hi, mid, lo, ones], axis=1)
        qa_ref[sl, :] = jnp.dot(lhs, selq_ref[...], preferred_element_type=F32).astype(BF16)
        ka_ref[sl, :] = jnp.dot(lhs, selk_ref[...], preferred_element_type=F32).astype(BF16)
        return c
    lax.fori_loop(0, s // blk, body, 0)


def _decay_selectors(n_heads):
    selq = np.zeros((4 * LANES, n_heads * LANES), np.float32)
    selk = np.zeros((4 * LANES, n_heads * LANES), np.float32)
    for h in range(n_heads):
        for p in range(3):
            selq[p * LANES + h, h * LANES + p] = 1.0
            selq[3 * LANES, h * LANES + 3 + p] = 1.0
            selk[3 * LANES, h * LANES + p] = 1.0
            selk[p * LANES + h, h * LANES + 3 + p] = -1.0
    return jnp.asarray(selq, BF16), jnp.asarray(selk, BF16)


def _decay(zf, bf_row, *, batch, seq, n_heads):
    selq, selk = _decay_selectors(n_heads)
    w = n_heads * LANES
    return pl.pallas_call(
        _decay_kernel,
        grid=(batch,),
        in_specs=[
            pl.BlockSpec((seq, LANES), lambda b: (b, 0)),
            pl.BlockSpec((1, LANES), lambda b: (0, 0)),
            pl.BlockSpec((4 * LANES, w), lambda b: (0, 0)),
            pl.BlockSpec((4 * LANES, w), lambda b: (0, 0)),
        ],
        out_specs=[
            pl.BlockSpec((seq, w), lambda b: (b, 0)),
            pl.BlockSpec((seq, w), lambda b: (b, 0)),
        ],
        out_shape=[
            jax.ShapeDtypeStruct((batch * seq, w), BF16),
            jax.ShapeDtypeStruct((batch * seq, w), BF16),
        ],
        scratch_shapes=[pltpu.VMEM((seq, LANES), F32)],
        compiler_params=_params("parallel"),
        name="decay",
    )(zf, bf_row, selq, selk)


def _attn_kernel(qtab, jtab, q_ref, qa_ref, k_ref, ka_ref, v_ref, o_ref,
                 m_ref, acc_ref, s0, s1, bm0, bm1, p0, p1, a0, a1, *, n_off, n_diag):
    seq = q_ref.shape[0]
    s_sl, bm_sl, p_sl, a_sl = (s0, s1), (bm0, bm1), (p0, p1), (a0, a1)
    m_ref[...] = jnp.full(m_ref.shape, -1e30, F32)
    acc_ref[...] = jnp.zeros(acc_ref.shape, F32)
    for slot in range(2):
        s_sl[slot][...] = jnp.full(s0.shape, -jnp.inf, F32)
        bm_sl[slot][...] = jnp.full(bm0.shape, -jnp.inf, F32)
        p_sl[slot][...] = jnp.zeros(p0.shape, BF16)
        a_sl[slot][...] = jnp.ones(a0.shape, F32)
    ones = jnp.ones((TK, HEAD_DIM), BF16)

    def q_rows(g):
        return pl.ds(pl.multiple_of(qtab[g] * TQ, TQ), TQ)

    def k_rows(g):
        return pl.ds(pl.multiple_of(jtab[g] * TK, TK), TK)

    def stage1(g, slot, masked):
        qr, kr = q_rows(g), k_rows(g)
        qp = jnp.concatenate([q_ref[qr, :], qa_ref[qr, :]], axis=1)
        kp = jnp.concatenate([k_ref[kr, :], ka_ref[kr, :]], axis=1)
        s = lax.dot_general(qp, kp, (((1,), (1,)), ((), ())), preferred_element_type=F32)
        if masked:
            r = lax.broadcasted_iota(jnp.int32, (TQ, TK), 0)
            c = lax.broadcasted_iota(jnp.int32, (TQ, TK), 1)
            s = jnp.where(c <= r, s, -jnp.inf)
        s_sl[slot][...] = s
        bm_sl[slot][...] = jnp.broadcast_to(jnp.max(s, axis=1, keepdims=True), (TQ, LANES))

    def stage2(g, slot):
        qr = q_rows(g)
        m_prev = m_ref[qr, :]
        m_new = jnp.maximum(m_prev, bm_sl[slot][...])
        a_sl[slot][...] = jnp.exp2(m_prev - m_new)
        s = s_sl[slot][...]
        p_sl[slot][...] = jnp.exp2(
            s - jnp.concatenate([m_new] * (TK // LANES), axis=1)).astype(BF16)
        m_ref[qr, :] = m_new

    def stage3(g, slot):
        qr, kr = q_rows(g), k_rows(g)
        vp = jnp.concatenate([v_ref[kr, :], ones], axis=1)
        pv = jnp.dot(p_sl[slot][...], vp, preferred_element_type=F32)
        alpha = a_sl[slot][...]
        acc_ref[qr, :] = acc_ref[qr, :] * jnp.concatenate([alpha, alpha], axis=1) + pv

    def body(i, c, base, masked):
        g0 = base + ATTN_UNROLL * i
        for u in range(ATTN_UNROLL):
            g = g0 + u
            stage3(jnp.maximum(g - 2, 0), u % 2)
            stage2(jnp.maximum(g - 1, 0), (u + 1) % 2)
            stage1(g, u % 2, masked)
        return c

    assert n_off % ATTN_UNROLL == 0 and n_diag % ATTN_UNROLL == 0 and ATTN_UNROLL % 2 == 0
    total = n_off + n_diag
    lax.fori_loop(0, n_off // ATTN_UNROLL, functools.partial(body, base=0, masked=False), 0)
    lax.fori_loop(0, n_diag // ATTN_UNROLL, functools.partial(body, base=n_off, masked=True), 0)
    stage3(total - 2, 0)
    stage2(total - 1, 1)
    stage3(total - 1, 1)

    def fin(r, c):
        sl = pl.ds(pl.multiple_of(r * TQ, TQ), TQ)
        acc = acc_ref[sl, :]
        o_ref[sl, :] = (acc[:, :HEAD_DIM] / acc[:, HEAD_DIM:]).astype(BF16)
        return c
    lax.fori_loop(0, seq // TQ, fin, 0)


def _attention(z, qa, ka, *, batch, seq, n_heads):
    assert TQ == TK
    nq = seq // TQ
    t = batch * seq
    off = [(i, j) for i in range(nq) for j in range(i)]
    diag = [(i, i) for i in range(nq)]
    assert len(diag) >= 2
    qtab = jnp.asarray([p[0] for p in off + diag], jnp.int32)
    jtab = jnp.asarray([p[1] for p in off + diag], jnp.int32)
    grid_spec = pltpu.PrefetchScalarGridSpec(
        num_scalar_prefetch=2,
        grid=(batch, n_heads),
        in_specs=[
            pl.BlockSpec((seq, HEAD_DIM), lambda b, h, qt, jt: (b, h)),
            pl.BlockSpec((seq, LANES), lambda b, h, qt, jt: (b, h)),
            pl.BlockSpec((seq, HEAD_DIM), lambda b, h, qt, jt: (b, n_heads + h)),
            pl.BlockSpec((seq, LANES), lambda b, h, qt, jt: (b, h)),
            pl.BlockSpec((seq, HEAD_DIM), lambda b, h, qt, jt: (b, 2 * n_heads + h)),
        ],
        out_specs=pl.BlockSpec((seq, HEAD_DIM), lambda b, h, qt, jt: (b, h)),
        scratch_shapes=[
            pltpu.VMEM((seq, LANES), F32),
            pltpu.VMEM((seq, 2 * HEAD_DIM), F32),
            pltpu.VMEM((TQ, TK), F32), pltpu.VMEM((TQ, TK), F32),
            pltpu.VMEM((TQ, LANES), F32), pltpu.VMEM((TQ, LANES), F32),
            pltpu.VMEM((TQ, TK), BF16), pltpu.VMEM((TQ, TK), BF16),
            pltpu.VMEM((TQ, LANES), F32), pltpu.VMEM((TQ, LANES), F32),
        ],
    )
    return pl.pallas_call(
        functools.partial(_attn_kernel, n_off=len(off), n_diag=len(diag)),
        grid_spec=grid_spec,
        out_shape=jax.ShapeDtypeStruct((t, n_heads * HEAD_DIM), BF16),
        compiler_params=_params("parallel", "parallel"),
        name="attention",
    )(qtab, jtab, z, qa, z, ka, z)


def _gmlp_kernel(zu_ref, zv_ref, lng_ref, lnb_ref, ws_ref, bs_ref, og_ref, o_ref,
                 w_sc, v_sc, gm_sc):
    tm = zu_ref.shape[0]
    n_heads = ws_ref.shape[0]

    @pl.when(pl.program_id(0) == 0)
    def _():
        r = lax.broadcasted_iota(jnp.int32, (CHUNK, CHUNK), 0)
        c = lax.broadcasted_iota(jnp.int32, (CHUNK, CHUNK), 1)
        for h in range(n_heads):
            w_sc[h] = jnp.where(c <= r, ws_ref[h], 0.0).astype(BF16)

    for cc in range(tm // CHUNK):
        rows = slice(cc * CHUNK, (cc + 1) * CHUNK)
        gv = _gelu(zv_ref[rows, :].astype(F32))
        mu = jnp.mean(gv, axis=-1, keepdims=True)
        xc = gv - mu
        var = jnp.mean(xc * xc, axis=-1, keepdims=True)
        v_sc[rows, :] = (xc * lax.rsqrt(var + EPS) * lng_ref[...] + lnb_ref[...]).astype(BF16)

    for cc in range(tm // CHUNK):
        rows = slice(cc * CHUNK, (cc + 1) * CHUNK)
        for h in range(n_heads):
            cols = slice(h * HEAD_DIM, (h + 1) * HEAD_DIM)
            mix = jnp.dot(w_sc[h], v_sc[rows, cols], preferred_element_type=F32) + bs_ref[h]
            gm_sc[rows, cols] = _gelu(zu_ref[rows, cols].astype(F32)) * mix

    for cc in range(tm // CHUNK):
        rows = slice(cc * CHUNK, (cc + 1) * CHUNK)
        o_ref[rows, :] = _rms(gm_sc[rows, :], og_ref[...]).astype(BF16)


def _gmlp(z, ln_g, ln_b, w_s, b_s3, out_g, *, d_attn, d_gmlp):
    t = z.shape[0]
    n_heads = w_s.shape[0]
    ucol = 3 * d_attn // d_gmlp
    return pl.pallas_call(
        _gmlp_kernel,
        grid=(t // TM_GMLP,),
        in_specs=[
            pl.BlockSpec((TM_GMLP, d_gmlp), lambda i: (i, ucol)),
            pl.BlockSpec((TM_GMLP, d_gmlp), lambda i: (i, ucol + 1)),
            pl.BlockSpec((1, d_gmlp), lambda i: (0, 0)),
            pl.BlockSpec((1, d_gmlp), lambda i: (0, 0)),
            pl.BlockSpec((n_heads, CHUNK, CHUNK), lambda i: (0, 0, 0)),
            pl.BlockSpec((n_heads, CHUNK, 1), lambda i: (0, 0, 0)),
            pl.BlockSpec((1, d_gmlp), lambda i: (0, 0)),
        ],
        out_specs=pl.BlockSpec((TM_GMLP, d_gmlp), lambda i: (i, 0)),
        out_shape=jax.ShapeDtypeStruct((t, d_gmlp), BF16),
        scratch_shapes=[
            pltpu.VMEM((n_heads, CHUNK, CHUNK), BF16),
            pltpu.VMEM((TM_GMLP, d_gmlp), BF16),
            pltpu.VMEM((TM_GMLP, d_gmlp), F32),
        ],
        compiler_params=_params("arbitrary"),
        name="gmlp",
    )(z, z, ln_g, ln_b, w_s, b_s3, out_g)


def _outproj_kernel(a_ref, gm_ref, x_ref, ag_ref, w_ref, o_ref, an_ref):
    tm = a_ref.shape[0]
    d_attn = a_ref.shape[1]

    def body(r, c):
        sl = pl.ds(pl.multiple_of(r * ROWS, ROWS), ROWS)
        an_ref[sl, :] = _rms(a_ref[sl, :].astype(F32), ag_ref[...]).astype(BF16)
        return c
    lax.fori_loop(0, tm // ROWS, body, 0)
    y = (jnp.dot(an_ref[...], w_ref[:d_attn, :], preferred_element_type=F32)
         + jnp.dot(gm_ref[...], w_ref[d_attn:, :], preferred_element_type=F32))
    o_ref[...] = x_ref[...] + y


def _outproj(attn, gmn, x2d, attn_g, w_out):
    t, d = x2d.shape
    d_attn = attn.shape[1]
    d_gmlp = gmn.shape[1]
    return pl.pallas_call(
        _outproj_kernel,
        grid=(t // TM_OUT,),
        in_specs=[
            pl.BlockSpec((TM_OUT, d_attn), lambda i: (i, 0)),
            pl.BlockSpec((TM_OUT, d_gmlp), lambda i: (i, 0)),
            pl.BlockSpec((TM_OUT, d), lambda i: (i, 0)),
            pl.BlockSpec((1, d_attn), lambda i: (0, 0)),
            pl.BlockSpec((d_attn + d_gmlp, d), lambda i: (0, 0)),
        ],
        out_specs=pl.BlockSpec((TM_OUT, d), lambda i: (i, 0)),
        out_shape=jax.ShapeDtypeStruct((t, d), F32),
        scratch_shapes=[pltpu.VMEM((TM_OUT, d_attn), BF16)],
        compiler_params=_params("parallel"),
        name="outproj",
    )(attn, gmn, x2d, attn_g, w_out)


def _ffn_kernel(x_ref, g_ref, w1_ref, w2_ref, gf_ref, o_ref, h_ref, *, final_norm):
    f = pl.program_id(1)
    tm = x_ref.shape[0]

    @pl.when(f == 0)
    def _():
        def body(r, c):
            sl = pl.ds(pl.multiple_of(r * ROWS, ROWS), ROWS)
            h_ref[sl, :] = _rms(x_ref[sl, :], g_ref[...]).astype(BF16)
            o_ref[sl, :] = jnp.zeros((ROWS, o_ref.shape[1]), F32)
            return c
        lax.fori_loop(0, tm // ROWS, body, 0)

    a = jnp.maximum(jnp.dot(h_ref[...], w1_ref[...], preferred_element_type=F32), 0.0)
    o_ref[...] += jnp.dot((a * a).astype(BF16), w2_ref[...], preferred_element_type=F32)

    @pl.when(f == pl.num_programs(1) - 1)
    def _():
        def body(r, c):
            sl = pl.ds(pl.multiple_of(r * ROWS, ROWS), ROWS)
            y = x_ref[sl, :] + o_ref[sl, :]
            if final_norm:
                y = _rms(y, gf_ref[...])
            o_ref[sl, :] = y
            return c
        lax.fori_loop(0, tm // ROWS, body, 0)


def _ffn(x2d, g, w1, w2, g_final, *, final_norm):
    t, d = x2d.shape
    d_ff = w1.shape[1]
    return pl.pallas_call(
        functools.partial(_ffn_kernel, final_norm=final_norm),
        grid=(t // TM_FFN, d_ff // TF_FFN),
        in_specs=[
            pl.BlockSpec((TM_FFN, d), lambda i, f: (i, 0)),
            pl.BlockSpec((1, d), lambda i, f: (0, 0)),
            pl.BlockSpec((d, TF_FFN), lambda i, f: (0, f)),
            pl.BlockSpec((TF_FFN, d), lambda i, f: (f, 0)),
            pl.BlockSpec((1, d), lambda i, f: (0, 0)),
        ],
        out_specs=pl.BlockSpec((TM_FFN, d), lambda i, f: (i, 0)),
        out_shape=jax.ShapeDtypeStruct((t, d), F32),
        scratch_shapes=[pltpu.VMEM((TM_FFN, d), BF16)],
        compiler_params=_params("parallel", "arbitrary"),
        name="ffn",
    )(x2d, g, w1, w2, g_final)


def kernel(x, norm_mix_g, w_in, b_f, gmlp_ln_g, gmlp_ln_b, w_s, b_s, attn_out_g, gmlp_out_g,
           w_out, norm_ffn_g, w_ff1, w_ff2, norm_final_g):
    batch, seq, d = x.shape
    depth = w_in.shape[0]
    n_heads = b_f.shape[1]
    d_attn = n_heads * HEAD_DIM
    d_gmlp = gmlp_ln_g.shape[1]
    o3 = 3 * d_attn
    o4 = o3 + n_heads
    assert d_attn == d_gmlp and w_in.shape[2] == o4 + 2 * d_gmlp

    x2d = x.reshape(batch * seq, d)
    for l in range(depth):
        w_main = jnp.concatenate([w_in[l][:, :o3], w_in[l][:, o4:]], axis=1).astype(BF16)
        w_f = jnp.pad(w_in[l][:, o3:o4], ((0, 0), (0, LANES - n_heads))).astype(BF16)
        bf_row = jnp.pad(b_f[l].astype(F32), (0, LANES - n_heads)).reshape(1, LANES)

        z, zf = _inproj(x2d, norm_mix_g[l].reshape(1, d), w_main, w_f, d_attn=d_attn)
        qa, ka = _decay(zf, bf_row, batch=batch, seq=seq, n_heads=n_heads)
        attn = _attention(z, qa, ka, batch=batch, seq=seq, n_heads=n_heads)
        gmn = _gmlp(z, gmlp_ln_g[l].reshape(1, d_gmlp), gmlp_ln_b[l].reshape(1, d_gmlp),
                    w_s[l], b_s[l].reshape(n_heads, CHUNK, 1), gmlp_out_g[l].reshape(1, d_gmlp),
                    d_attn=d_attn, d_gmlp=d_gmlp)
        x2d = _outproj(attn, gmn, x2d, attn_out_g[l].reshape(1, d_attn), w_out[l].astype(BF16))
        x2d = _ffn(x2d, norm_ffn_g[l].reshape(1, d), w_ff1[l].astype(BF16), w_ff2[l].astype(BF16),
                   norm_final_g.reshape(1, d), final_norm=(l == depth - 1))
    return x2d.reshape(batch, seq, d)
```

```python
import functools
import math

import numpy as np
import jax
import jax.numpy as jnp
from jax import lax
from jax.experimental import pallas as pl
from jax.experimental.pallas import tpu as pltpu

F32 = jnp.float32
BF16 = jnp.bfloat16

HEAD_DIM = 128
CHUNK = 128
EPS = 1e-6
LANES = 128
LOG2E = 1.4426950408889634
VMEM_LIMIT = 62 * 1024 * 1024

TM_PROJ = 1024
TN_PROJ = 1024
TM_DECAY = 1024
TQ = 512
TK = 512
ATTN_UNROLL = 4
TM_GMLP = 512
TM_OUT = 512
TM_FFN = 1024
TF_FFN = 1024
ROWS = 128
CUM_BLK = 256
DECAY_LANES = 8


def _rms(x, g):
    ms = jnp.mean(x * x, axis=-1, keepdims=True)
    return x * lax.rsqrt(ms + EPS) * g


def _gelu(x):
    c = math.sqrt(2.0 / math.pi)
    return x * (0.5 * (1.0 + jnp.tanh(c * (x + 0.044715 * (x * x * x)))))


def _split3(x):
    hi = x.astype(BF16)
    r1 = x - hi.astype(F32)
    mid = r1.astype(BF16)
    lo = (r1 - mid.astype(F32)).astype(BF16)
    return hi, mid, lo


def _params(*sem):
    return pltpu.CompilerParams(dimension_semantics=sem, vmem_limit_bytes=VMEM_LIMIT)


def _inproj_kernel(x_ref, g_ref, w_ref, wf_ref, z_ref, zf_ref, h_ref, *, n_q_tiles, q_scale):
    j = pl.program_id(1)
    tm = x_ref.shape[0]

    def project():
        acc = jnp.dot(h_ref[...], w_ref[...], preferred_element_type=F32)
        scale = jnp.where(j < n_q_tiles, q_scale, 1.0).astype(F32)
        z_ref[...] = (acc * scale).astype(BF16)

    @pl.when(j == 0)
    def _():
        for r in range(tm // ROWS):
            sl = slice(r * ROWS, (r + 1) * ROWS)
            h_ref[sl, :] = _rms(x_ref[sl, :], g_ref[...]).astype(BF16)
        zf_ref[...] = jnp.dot(h_ref[...], wf_ref[...], preferred_element_type=F32)
        project()

    @pl.when(j > 0)
    def _():
        project()


def _inproj(x2d, g, w_main, w_f, *, d_attn):
    t, d = x2d.shape
    n = w_main.shape[1]
    q_scale = LOG2E / math.sqrt(HEAD_DIM)
    return pl.pallas_call(
        functools.partial(_inproj_kernel, n_q_tiles=d_attn // TN_PROJ, q_scale=q_scale),
        grid=(t // TM_PROJ, n // TN_PROJ),
        in_specs=[
            pl.BlockSpec((TM_PROJ, d), lambda i, j: (i, 0)),
            pl.BlockSpec((1, d), lambda i, j: (0, 0)),
            pl.BlockSpec((d, TN_PROJ), lambda i, j: (0, j)),
            pl.BlockSpec((d, LANES), lambda i, j: (0, 0)),
        ],
        out_specs=[
            pl.BlockSpec((TM_PROJ, TN_PROJ), lambda i, j: (i, j)),
            pl.BlockSpec((TM_PROJ, LANES), lambda i, j: (i, 0)),
        ],
        out_shape=[
            jax.ShapeDtypeStruct((t, n), BF16),
            jax.ShapeDtypeStruct((t, LANES), F32),
        ],
        scratch_shapes=[pltpu.VMEM((TM_PROJ, d), BF16)],
        compiler_params=_params("parallel", "arbitrary"),
        name="inproj",
    )(x2d, g, w_main, w_f)


def _decay_kernel(zf_ref, bf_ref, qa_ref, ka_ref, f_ref, carry_ref, *, n_heads):
    rows = zf_ref.shape[0]

    @pl.when(pl.program_id(1) == 0)
    def _():
        carry_ref[...] = jnp.zeros(carry_ref.shape, F32)

    row = lax.broadcasted_iota(jnp.int32, (CUM_BLK, CUM_BLK), 0)
    col = lax.broadcasted_iota(jnp.int32, (CUM_BLK, CUM_BLK), 1)
    tri = (col <= row).astype(BF16)
    carry = carry_ref[...]
    for r in range(rows // CUM_BLK):
        sl = slice(r * CUM_BLK, (r + 1) * CUM_BLK)
        x = zf_ref[sl, :] + bf_ref[...]
        logf = jnp.minimum(x, 0.0) - jnp.log1p(jnp.exp(-jnp.abs(x)))
        hi, mid, lo = _split3(logf)
        c = (jnp.dot(tri, hi, preferred_element_type=F32)
             + jnp.dot(tri, mid, preferred_element_type=F32)
             + jnp.dot(tri, lo, preferred_element_type=F32)) + carry
        f_ref[sl, :] = c
        carry = c[CUM_BLK - 1:CUM_BLK, :]
    carry_ref[...] = carry

    hi, mid, lo = _split3(f_ref[...] * LOG2E)
    lane = lax.broadcasted_iota(jnp.int32, (rows, LANES), 1)
    sub = lane & (DECAY_LANES - 1)
    used = lane < n_heads * DECAY_LANES
    piece_idx = jnp.where(sub >= 3, sub - 3, sub)
    piece = jnp.where(piece_idx == 0, hi, jnp.where(piece_idx == 1, mid, lo)).astype(F32)
    first = jnp.logical_and(used, sub < 3)
    second = jnp.logical_and(used, jnp.logical_and(sub >= 3, sub < 6))
    qa_ref[...] = jnp.where(first, piece, jnp.where(second, 1.0, 0.0)).astype(BF16)
    ka_ref[...] = jnp.where(first, 1.0, jnp.where(second, -piece, 0.0)).astype(BF16)


def _decay_lanes(v, n_heads):
    lead = v.shape[:-1]
    rep = jnp.broadcast_to(v[..., None], lead + (n_heads, 6))
    rep = jnp.pad(rep, [(0, 0)] * len(lead) + [(0, 0), (0, DECAY_LANES - 6)])
    rep = rep.reshape(lead + (n_heads * DECAY_LANES,))
    return jnp.pad(rep, [(0, 0)] * len(lead) + [(0, LANES - n_heads * DECAY_LANES)])


def _decay(zf, bf_row, *, batch, seq, n_heads):
    assert n_heads * DECAY_LANES <= LANES
    ns = seq // TM_DECAY
    return pl.pallas_call(
        functools.partial(_decay_kernel, n_heads=n_heads),
        grid=(batch, ns),
        in_specs=[
            pl.BlockSpec((TM_DECAY, LANES), lambda b, s: (b * ns + s, 0)),
            pl.BlockSpec((1, LANES), lambda b, s: (0, 0)),
        ],
        out_specs=[
            pl.BlockSpec((TM_DECAY, LANES), lambda b, s: (b * ns + s, 0)),
            pl.BlockSpec((TM_DECAY, LANES), lambda b, s: (b * ns + s, 0)),
        ],
        out_shape=[
            jax.ShapeDtypeStruct((batch * seq, LANES), BF16),
            jax.ShapeDtypeStruct((batch * seq, LANES), BF16),
        ],
        scratch_shapes=[pltpu.VMEM((TM_DECAY, LANES), F32), pltpu.VMEM((1, LANES), F32)],
        compiler_params=_params("parallel", "arbitrary"),
        name="decay",
    )(zf, bf_row)


def _attn_kernel(qtab, jtab, q_ref, qa_ref, k_ref, ka_ref, v_ref, o_ref,
                 qam_ref, m_ref, acc_ref, s0, s1, bm0, bm1, p0, p1, a0, a1, *, n_off, n_diag):
    seq = q_ref.shape[0]
    s_sl, bm_sl, p_sl, a_sl = (s0, s1), (bm0, bm1), (p0, p1), (a0, a1)
    lane = lax.broadcasted_iota(jnp.int32, qa_ref.shape, 1)
    head_lane0 = pl.program_id(1) * DECAY_LANES
    mine = jnp.logical_and(lane >= head_lane0, lane < head_lane0 + DECAY_LANES)
    qam_ref[...] = jnp.where(mine, qa_ref[...], jnp.zeros(qa_ref.shape, BF16))
    m_ref[...] = jnp.full(m_ref.shape, -1e30, F32)
    acc_ref[...] = jnp.zeros(acc_ref.shape, F32)
    for slot in range(2):
        s_sl[slot][...] = jnp.full(s0.shape, -jnp.inf, F32)
        bm_sl[slot][...] = jnp.full(bm0.shape, -jnp.inf, F32)
        p_sl[slot][...] = jnp.zeros(p0.shape, BF16)
        a_sl[slot][...] = jnp.ones(a0.shape, F32)
    ones = jnp.ones((TK, HEAD_DIM), BF16)

    def q_rows(g):
        return pl.ds(pl.multiple_of(qtab[g] * TQ, TQ), TQ)

    def k_rows(g):
        return pl.ds(pl.multiple_of(jtab[g] * TK, TK), TK)

    def stage1(g, slot, masked):
        qr, kr = q_rows(g), k_rows(g)
        qp = jnp.concatenate([q_ref[qr, :], qam_ref[qr, :]], axis=1)
        kp = jnp.concatenate([k_ref[kr, :], ka_ref[kr, :]], axis=1)
        s = lax.dot_general(qp, kp, (((1,), (1,)), ((), ())), preferred_element_type=F32)
        if masked:
            r = lax.broadcasted_iota(jnp.int32, (TQ, TK), 0)
            c = lax.broadcasted_iota(jnp.int32, (TQ, TK), 1)
            s = jnp.where(c <= r, s, -jnp.inf)
        s_sl[slot][...] = s
        bm_sl[slot][...] = jnp.broadcast_to(jnp.max(s, axis=1, keepdims=True), (TQ, LANES))

    def stage2(g, slot):
        qr = q_rows(g)
        m_prev = m_ref[qr, :]
        m_new = jnp.maximum(m_prev, bm_sl[slot][...])
        a_sl[slot][...] = jnp.exp2(m_prev - m_new)
        s = s_sl[slot][...]
        p_sl[slot][...] = jnp.exp2(
            s - jnp.concatenate([m_new] * (TK // LANES), axis=1)).astype(BF16)
        m_ref[qr, :] = m_new

    def stage3(g, slot):
        qr, kr = q_rows(g), k_rows(g)
        vp = jnp.concatenate([v_ref[kr, :], ones], axis=1)
        pv = jnp.dot(p_sl[slot][...], vp, preferred_element_type=F32)
        alpha = a_sl[slot][...]
        acc_ref[qr, :] = acc_ref[qr, :] * jnp.concatenate([alpha, alpha], axis=1) + pv

    def body(i, c, base, masked):
        g0 = base + ATTN_UNROLL * i
        for u in range(ATTN_UNROLL):
            g = g0 + u
            stage3(jnp.maximum(g - 2, 0), u % 2)
            stage2(jnp.maximum(g - 1, 0), (u + 1) % 2)
            stage1(g, u % 2, masked)
        return c

    assert n_off % ATTN_UNROLL == 0 and n_diag % ATTN_UNROLL == 0 and ATTN_UNROLL % 2 == 0
    total = n_off + n_diag
    lax.fori_loop(0, n_off // ATTN_UNROLL, functools.partial(body, base=0, masked=False), 0)
    lax.fori_loop(0, n_diag // ATTN_UNROLL, functools.partial(body, base=n_off, masked=True), 0)
    stage3(total - 2, 0)
    stage2(total - 1, 1)
    stage3(total - 1, 1)

    def fin(r, c):
        sl = pl.ds(pl.multiple_of(r * TQ, TQ), TQ)
        acc = acc_ref[sl, :]
        o_ref[sl, :] = (acc[:, :HEAD_DIM] / acc[:, HEAD_DIM:]).astype(BF16)
        return c
    lax.fori_loop(0, seq // TQ, fin, 0)


def _attention(z, qa, ka, *, batch, seq, n_heads):
    assert TQ == TK
    nq = seq // TQ
    t = batch * seq
    off = [(i, j) for i in range(nq) for j in range(i)]
    diag = [(i, i) for i in range(nq)]
    assert len(diag) >= 2
    qtab = jnp.asarray([p[0] for p in off + diag], jnp.int32)
    jtab = jnp.asarray([p[1] for p in off + diag], jnp.int32)
    grid_spec = pltpu.PrefetchScalarGridSpec(
        num_scalar_prefetch=2,
        grid=(batch, n_heads),
        in_specs=[
            pl.BlockSpec((seq, HEAD_DIM), lambda b, h, qt, jt: (b, h)),
            pl.BlockSpec((seq, LANES), lambda b, h, qt, jt: (b, 0)),
            pl.BlockSpec((seq, HEAD_DIM), lambda b, h, qt, jt: (b, n_heads + h)),
            pl.BlockSpec((seq, LANES), lambda b, h, qt, jt: (b, 0)),
            pl.BlockSpec((seq, HEAD_DIM), lambda b, h, qt, jt: (b, 2 * n_heads + h)),
        ],
        out_specs=pl.BlockSpec((seq, HEAD_DIM), lambda b, h, qt, jt: (b, h)),
        scratch_shapes=[
            pltpu.VMEM((seq, LANES), BF16),
            pltpu.VMEM((seq, LANES), F32),
            pltpu.VMEM((seq, 2 * HEAD_DIM), F32),
            pltpu.VMEM((TQ, TK), F32), pltpu.VMEM((TQ, TK), F32),
            pltpu.VMEM((TQ, LANES), F32), pltpu.VMEM((TQ, LANES), F32),
            pltpu.VMEM((TQ, TK), BF16), pltpu.VMEM((TQ, TK), BF16),
            pltpu.VMEM((TQ, LANES), F32), pltpu.VMEM((TQ, LANES), F32),
        ],
    )
    return pl.pallas_call(
        functools.partial(_attn_kernel, n_off=len(off), n_diag=len(diag)),
        grid_spec=grid_spec,
        out_shape=jax.ShapeDtypeStruct((t, n_heads * HEAD_DIM), BF16),
        compiler_params=_params("parallel", "parallel"),
        name="attention",
    )(qtab, jtab, z, qa, z, ka, z)


def _gmlp_kernel(zu_ref, zv_ref, lng_ref, lnb_ref, ws_ref, bs_ref, og_ref, o_ref,
                 w_sc, v_sc, gm_sc):
    tm = zu_ref.shape[0]
    n_heads = ws_ref.shape[0]

    @pl.when(pl.program_id(0) == 0)
    def _():
        r = lax.broadcasted_iota(jnp.int32, (CHUNK, CHUNK), 0)
        c = lax.broadcasted_iota(jnp.int32, (CHUNK, CHUNK), 1)
        for h in range(n_heads):
            w_sc[h] = jnp.where(c <= r, ws_ref[h], 0.0).astype(BF16)

    for cc in range(tm // CHUNK):
        rows = slice(cc * CHUNK, (cc + 1) * CHUNK)
        gv = _gelu(zv_ref[rows, :].astype(F32))
        mu = jnp.mean(gv, axis=-1, keepdims=True)
        xc = gv - mu
        var = jnp.mean(xc * xc, axis=-1, keepdims=True)
        v_sc[rows, :] = (xc * lax.rsqrt(var + EPS) * lng_ref[...] + lnb_ref[...]).astype(BF16)

    for cc in range(tm // CHUNK):
        rows = slice(cc * CHUNK, (cc + 1) * CHUNK)
        for h in range(n_heads):
            cols = slice(h * HEAD_DIM, (h + 1) * HEAD_DIM)
            mix = jnp.dot(w_sc[h], v_sc[rows, cols], preferred_element_type=F32) + bs_ref[h]
            gm_sc[rows, cols] = _gelu(zu_ref[rows, cols].astype(F32)) * mix

    for cc in range(tm // CHUNK):
        rows = slice(cc * CHUNK, (cc + 1) * CHUNK)
        o_ref[rows, :] = _rms(gm_sc[rows, :], og_ref[...]).astype(BF16)


def _gmlp(z, ln_g, ln_b, w_s, b_s3, out_g, *, d_attn, d_gmlp):
    t = z.shape[0]
    n_heads = w_s.shape[0]
    ucol = 3 * d_attn // d_gmlp
    return pl.pallas_call(
        _gmlp_kernel,
        grid=(t // TM_GMLP,),
        in_specs=[
            pl.BlockSpec((TM_GMLP, d_gmlp), lambda i: (i, ucol)),
            pl.BlockSpec((TM_GMLP, d_gmlp), lambda i: (i, ucol + 1)),
            pl.BlockSpec((1, d_gmlp), lambda i: (0, 0)),
            pl.BlockSpec((1, d_gmlp), lambda i: (0, 0)),
            pl.BlockSpec((n_heads, CHUNK, CHUNK), lambda i: (0, 0, 0)),
            pl.BlockSpec((n_heads, CHUNK, 1), lambda i: (0, 0, 0)),
            pl.BlockSpec((1, d_gmlp), lambda i: (0, 0)),
        ],
        out_specs=pl.BlockSpec((TM_GMLP, d_gmlp), lambda i: (i, 0)),
        out_shape=jax.ShapeDtypeStruct((t, d_gmlp), BF16),
        scratch_shapes=[
            pltpu.VMEM((n_heads, CHUNK, CHUNK), BF16),
            pltpu.VMEM((TM_GMLP, d_gmlp), BF16),
            pltpu.VMEM((TM_GMLP, d_gmlp), F32),
        ],
        compiler_params=_params("arbitrary"),
        name="gmlp",
    )(z, z, ln_g, ln_b, w_s, b_s3, out_g)


def _outproj_kernel(a_ref, gm_ref, x_ref, ag_ref, w_ref, o_ref, an_ref):
    tm = a_ref.shape[0]
    d_attn = a_ref.shape[1]

    for r in range(tm // ROWS):
        sl = slice(r * ROWS, (r + 1) * ROWS)
        an_ref[sl, :] = _rms(a_ref[sl, :].astype(F32), ag_ref[...]).astype(BF16)
    y = (jnp.dot(an_ref[...], w_ref[:d_attn, :], preferred_element_type=F32)
         + jnp.dot(gm_ref[...], w_ref[d_attn:, :], preferred_element_type=F32))
    o_ref[...] = x_ref[...] + y


def _outproj(attn, gmn, x2d, attn_g, w_out):
    t, d = x2d.shape
    d_attn = attn.shape[1]
    d_gmlp = gmn.shape[1]
    return pl.pallas_call(
        _outproj_kernel,
        grid=(t // TM_OUT,),
        in_specs=[
            pl.BlockSpec((TM_OUT, d_attn), lambda i: (i, 0)),
            pl.BlockSpec((TM_OUT, d_gmlp), lambda i: (i, 0)),
            pl.BlockSpec((TM_OUT, d), lambda i: (i, 0)),
            pl.BlockSpec((1, d_attn), lambda i: (0, 0)),
            pl.BlockSpec((d_attn + d_gmlp, d), lambda i: (0, 0)),
        ],
        out_specs=pl.BlockSpec((TM_OUT, d), lambda i: (i, 0)),
        out_shape=jax.ShapeDtypeStruct((t, d), F32),
        scratch_shapes=[pltpu.VMEM((TM_OUT, d_attn), BF16)],
        compiler_params=_params("parallel"),
        name="outproj",
    )(attn, gmn, x2d, attn_g, w_out)


def _ffn_kernel(x_ref, g_ref, w1_ref, w2_ref, gf_ref, o_ref, h_ref, *, final_norm):
    f = pl.program_id(1)
    last = pl.num_programs(1) - 1
    tm = x_ref.shape[0]

    def mlp():
        a = jnp.maximum(jnp.dot(h_ref[...], w1_ref[...], preferred_element_type=F32), 0.0)
        return jnp.dot((a * a).astype(BF16), w2_ref[...], preferred_element_type=F32)

    @pl.when(f == 0)
    def _():
        for r in range(tm // ROWS):
            sl = slice(r * ROWS, (r + 1) * ROWS)
            h_ref[sl, :] = _rms(x_ref[sl, :], g_ref[...]).astype(BF16)
        o_ref[...] = mlp()

    @pl.when(jnp.logical_and(f > 0, f < last))
    def _():
        o_ref[...] += mlp()

    @pl.when(f == last)
    def _():
        y = x_ref[...] + (o_ref[...] + mlp())
        if final_norm:
            for r in range(tm // ROWS):
                sl = slice(r * ROWS, (r + 1) * ROWS)
                o_ref[sl, :] = _rms(y[sl, :], gf_ref[...])
        else:
            o_ref[...] = y


def _ffn(x2d, g, w1, w2, g_final, *, final_norm):
    t, d = x2d.shape
    d_ff = w1.shape[1]
    assert d_ff // TF_FFN >= 2
    return pl.pallas_call(
        functools.partial(_ffn_kernel, final_norm=final_norm),
        grid=(t // TM_FFN, d_ff // TF_FFN),
        in_specs=[
            pl.BlockSpec((TM_FFN, d), lambda i, f: (i, 0)),
            pl.BlockSpec((1, d), lambda i, f: (0, 0)),
            pl.BlockSpec((d, TF_FFN), lambda i, f: (0, f)),
            pl.BlockSpec((TF_FFN, d), lambda i, f: (f, 0)),
            pl.BlockSpec((1, d), lambda i, f: (0, 0)),
        ],
        out_specs=pl.BlockSpec((TM_FFN, d), lambda i, f: (i, 0)),
        out_shape=jax.ShapeDtypeStruct((t, d), F32),
        scratch_shapes=[pltpu.VMEM((TM_FFN, d), BF16)],
        compiler_params=_params("parallel", "arbitrary"),
        name="ffn",
    )(x2d, g, w1, w2, g_final)


def kernel(x, norm_mix_g, w_in, b_f, gmlp_ln_g, gmlp_ln_b, w_s, b_s, attn_out_g, gmlp_out_g,
           w_out, norm_ffn_g, w_ff1, w_ff2, norm_final_g):
    batch, seq, d = x.shape
    depth = w_in.shape[0]
    n_heads = b_f.shape[1]
    d_attn = n_heads * HEAD_DIM
    d_gmlp = gmlp_ln_g.shape[1]
    o3 = 3 * d_attn
    o4 = o3 + n_heads
    assert d_attn == d_gmlp and w_in.shape[2] == o4 + 2 * d_gmlp

    x2d = x.reshape(batch * seq, d)
    for l in range(depth):
        w16 = w_in[l].astype(BF16)
        w_main = jnp.concatenate([w16[:, :o3], w16[:, o4:]], axis=1)
        w_f = _decay_lanes(w16[:, o3:o4], n_heads)
        bf_row = _decay_lanes(b_f[l].astype(F32), n_heads).reshape(1, LANES)

        z, zf = _inproj(x2d, norm_mix_g[l].reshape(1, d), w_main, w_f, d_attn=d_attn)
        qa, ka = _decay(zf, bf_row, batch=batch, seq=seq, n_heads=n_heads)
        attn = _attention(z, qa, ka, batch=batch, seq=seq, n_heads=n_heads)
        gmn = _gmlp(z, gmlp_ln_g[l].reshape(1, d_gmlp), gmlp_ln_b[l].reshape(1, d_gmlp),
                    w_s[l], b_s[l].reshape(n_heads, CHUNK, 1), gmlp_out_g[l].reshape(1, d_gmlp),
                    d_attn=d_attn, d_gmlp=d_gmlp)
        x2d = _outproj(attn, gmn, x2d, attn_out_g[l].reshape(1, d_attn), w_out[l].astype(BF16))
        x2d = _ffn(x2d, norm_ffn_g[l].reshape(1, d), w_ff1[l].astype(BF16), w_ff2[l].astype(BF16),
                   norm_final_g.reshape(1, d), final_norm=(l == depth - 1))
    return x2d.reshape(batch, seq, d)
```

```python
import functools
import math

import numpy as np
import jax
import jax.numpy as jnp
from jax import lax
from jax.experimental import pallas as pl
from jax.experimental.pallas import tpu as pltpu

F32 = jnp.float32
BF16 = jnp.bfloat16

HEAD_DIM = 128
CHUNK = 128
EPS = 1e-6
LANES = 128
LOG2E = 1.4426950408889634
VMEM_LIMIT = 62 * 1024 * 1024

TM_WPREP = 256
TM_PROJ = 1024
TN_PROJ = 1024
TM_DECAY = 1024
TQ = 512
TK = 512
ATTN_SLOTS = 4
TM_GMLP = 512
TM_OUT = 512
TM_FFN = 1024
TF_FFN = 1024
ROWS = 128
CUM_BLK = 256
DECAY_LANES = 8


def _rms(x, g):
    ms = jnp.mean(x * x, axis=-1, keepdims=True)
    return x * lax.rsqrt(ms + EPS) * g


def _gelu(x):
    c = math.sqrt(2.0 / math.pi)
    return x * (0.5 * (1.0 + jnp.tanh(c * (x + 0.044715 * (x * x * x)))))


def _split3(x):
    hi = x.astype(BF16)
    r1 = x - hi.astype(F32)
    mid = r1.astype(BF16)
    lo = (r1 - mid.astype(F32)).astype(BF16)
    return hi, mid, lo


def _params(*sem):
    return pltpu.CompilerParams(dimension_semantics=sem, vmem_limit_bytes=VMEM_LIMIT)


def _wprep_kernel(w_ref, main_ref, gate_ref, *, o3, o4):
    n_main = main_ref.shape[1]
    main_ref[:, :o3] = w_ref[:, :o3].astype(BF16)
    main_ref[:, o3:] = w_ref[:, o4:o4 + n_main - o3].astype(BF16)
    gate_ref[...] = w_ref[:, o3:o3 + LANES].astype(BF16)


def _wprep(w, *, o3, o4):
    d, n = w.shape
    n_main = n - (o4 - o3)
    return pl.pallas_call(
        functools.partial(_wprep_kernel, o3=o3, o4=o4),
        grid=(d // TM_WPREP,),
        in_specs=[pl.BlockSpec((TM_WPREP, n), lambda i: (i, 0))],
        out_specs=[
            pl.BlockSpec((TM_WPREP, n_main), lambda i: (i, 0)),
            pl.BlockSpec((TM_WPREP, LANES), lambda i: (i, 0)),
        ],
        out_shape=[
            jax.ShapeDtypeStruct((d, n_main), BF16),
            jax.ShapeDtypeStruct((d, LANES), BF16),
        ],
        compiler_params=_params("parallel"),
        name="wprep",
    )(w)


def _inproj_kernel(x_ref, g_ref, w_ref, wf_ref, z_ref, zf_ref, h_ref, *, n_q_tiles, q_scale):
    j = pl.program_id(1)
    tm = x_ref.shape[0]

    def project():
        acc = jnp.dot(h_ref[...], w_ref[...], preferred_element_type=F32)
        scale = jnp.where(j < n_q_tiles, q_scale, 1.0).astype(F32)
        z_ref[...] = (acc * scale).astype(BF16)

    @pl.when(j == 0)
    def _():
        for r in range(tm // ROWS):
            sl = slice(r * ROWS, (r + 1) * ROWS)
            h_ref[sl, :] = _rms(x_ref[sl, :], g_ref[...]).astype(BF16)
        zf_ref[...] = jnp.dot(h_ref[...], wf_ref[...], preferred_element_type=F32)
        project()

    @pl.when(j > 0)
    def _():
        project()


def _inproj(x2d, g, w_main, w_f, *, d_attn):
    t, d = x2d.shape
    n = w_main.shape[1]
    q_scale = LOG2E / math.sqrt(HEAD_DIM)
    return pl.pallas_call(
        functools.partial(_inproj_kernel, n_q_tiles=d_attn // TN_PROJ, q_scale=q_scale),
        grid=(t // TM_PROJ, n // TN_PROJ),
        in_specs=[
            pl.BlockSpec((TM_PROJ, d), lambda i, j: (i, 0)),
            pl.BlockSpec((1, d), lambda i, j: (0, 0)),
            pl.BlockSpec((d, TN_PROJ), lambda i, j: (0, j)),
            pl.BlockSpec((d, LANES), lambda i, j: (0, 0)),
        ],
        out_specs=[
            pl.BlockSpec((TM_PROJ, TN_PROJ), lambda i, j: (i, j)),
            pl.BlockSpec((TM_PROJ, LANES), lambda i, j: (i, 0)),
        ],
        out_shape=[
            jax.ShapeDtypeStruct((t, n), BF16),
            jax.ShapeDtypeStruct((t, LANES), F32),
        ],
        scratch_shapes=[pltpu.VMEM((TM_PROJ, d), BF16)],
        compiler_params=_params("parallel", "arbitrary"),
        name="inproj",
    )(x2d, g, w_main, w_f)


def _decay_kernel(zf_ref, bf_ref, qa_ref, ka_ref, f_ref, carry_ref, *, n_heads):
    rows = zf_ref.shape[0]

    @pl.when(pl.program_id(1) == 0)
    def _():
        carry_ref[...] = jnp.zeros(carry_ref.shape, F32)

    row = lax.broadcasted_iota(jnp.int32, (CUM_BLK, CUM_BLK), 0)
    col = lax.broadcasted_iota(jnp.int32, (CUM_BLK, CUM_BLK), 1)
    tri = (col <= row).astype(BF16)
    carry = carry_ref[...]
    for r in range(rows // CUM_BLK):
        sl = slice(r * CUM_BLK, (r + 1) * CUM_BLK)
        x = zf_ref[sl, :] + bf_ref[...]
        logf = jnp.minimum(x, 0.0) - jnp.log1p(jnp.exp(-jnp.abs(x)))
        hi, mid, lo = _split3(logf)
        c = (jnp.dot(tri, hi, preferred_element_type=F32)
             + jnp.dot(tri, mid, preferred_element_type=F32)
             + jnp.dot(tri, lo, preferred_element_type=F32)) + carry
        f_ref[sl, :] = c
        carry = c[CUM_BLK - 1:CUM_BLK, :]
    carry_ref[...] = carry

    hi, mid, lo = _split3(f_ref[...] * LOG2E)
    lane = lax.broadcasted_iota(jnp.int32, (rows, LANES), 1)
    sub = lane & (DECAY_LANES - 1)
    used = lane < n_heads * DECAY_LANES
    piece_idx = jnp.where(sub >= 3, sub - 3, sub)
    piece = jnp.where(piece_idx == 0, hi, jnp.where(piece_idx == 1, mid, lo)).astype(F32)
    first = jnp.logical_and(used, sub < 3)
    second = jnp.logical_and(used, jnp.logical_and(sub >= 3, sub < 6))
    qa_ref[...] = jnp.where(first, piece, jnp.where(second, 1.0, 0.0)).astype(BF16)
    ka_ref[...] = jnp.where(first, 1.0, jnp.where(second, -piece, 0.0)).astype(BF16)


def _decay_lanes(v, n_heads):
    lead = v.shape[:-1]
    rep = jnp.broadcast_to(v[..., None], lead + (n_heads, 6))
    rep = jnp.pad(rep, [(0, 0)] * len(lead) + [(0, 0), (0, DECAY_LANES - 6)])
    rep = rep.reshape(lead + (n_heads * DECAY_LANES,))
    return jnp.pad(rep, [(0, 0)] * len(lead) + [(0, LANES - n_heads * DECAY_LANES)])


def _decay(zf, bf_row, *, batch, seq, n_heads):
    assert n_heads * DECAY_LANES <= LANES
    ns = seq // TM_DECAY
    return pl.pallas_call(
        functools.partial(_decay_kernel, n_heads=n_heads),
        grid=(batch, ns),
        in_specs=[
            pl.BlockSpec((TM_DECAY, LANES), lambda b, s: (b * ns + s, 0)),
            pl.BlockSpec((1, LANES), lambda b, s: (0, 0)),
        ],
        out_specs=[
            pl.BlockSpec((TM_DECAY, LANES), lambda b, s: (b * ns + s, 0)),
            pl.BlockSpec((TM_DECAY, LANES), lambda b, s: (b * ns + s, 0)),
        ],
        out_shape=[
            jax.ShapeDtypeStruct((batch * seq, LANES), BF16),
            jax.ShapeDtypeStruct((batch * seq, LANES), BF16),
        ],
        scratch_shapes=[pltpu.VMEM((TM_DECAY, LANES), F32), pltpu.VMEM((1, LANES), F32)],
        compiler_params=_params("parallel", "arbitrary"),
        name="decay",
    )(zf, bf_row)


def _attn_kernel(qtab, jtab, q_ref, qa_ref, k_ref, ka_ref, v_ref, *rest, n_off, n_diag, n_cast):
    cast_in, rest = rest[:n_cast], rest[n_cast:]
    o_ref, rest = rest[0], rest[1:]
    cast_out, rest = rest[:n_cast], rest[n_cast:]
    for src, dst in zip(cast_in, cast_out):
        dst[...] = src[...].astype(BF16)
    _attn_body(qtab, jtab, q_ref, qa_ref, k_ref, ka_ref, v_ref, o_ref, *rest,
               n_off=n_off, n_diag=n_diag)


def _attn_body(qtab, jtab, q_ref, qa_ref, k_ref, ka_ref, v_ref, o_ref,
               qam_ref, m_ref, acc_ref, s_ref, mn_ref, a_ref, *, n_off, n_diag):
    seq = q_ref.shape[0]
    lane = lax.broadcasted_iota(jnp.int32, qa_ref.shape, 1)
    head_lane0 = pl.program_id(1) * DECAY_LANES
    mine = jnp.logical_and(lane >= head_lane0, lane < head_lane0 + DECAY_LANES)
    qam_ref[...] = jnp.where(mine, qa_ref[...], jnp.zeros(qa_ref.shape, BF16))
    m_ref[...] = jnp.full(m_ref.shape, -1e30, F32)
    acc_ref[...] = jnp.zeros(acc_ref.shape, F32)
    for slot in (ATTN_SLOTS - 2, ATTN_SLOTS - 1):
        s_ref[slot] = jnp.full((TQ, TK), -jnp.inf, F32)
        mn_ref[slot] = jnp.zeros((TQ, LANES), F32)
        a_ref[slot] = jnp.ones((TQ, LANES), F32)
    ones = jnp.ones((TK, HEAD_DIM), BF16)

    def q_rows(g):
        return pl.ds(pl.multiple_of(qtab[g] * TQ, TQ), TQ)

    def k_rows(g):
        return pl.ds(pl.multiple_of(jtab[g] * TK, TK), TK)

    def stage1(g, slot, masked):
        qr, kr = q_rows(g), k_rows(g)
        qp = jnp.concatenate([q_ref[qr, :], qam_ref[qr, :]], axis=1)
        kp = jnp.concatenate([k_ref[kr, :], ka_ref[kr, :]], axis=1)
        s = lax.dot_general(qp, kp, (((1,), (1,)), ((), ())), preferred_element_type=F32)
        if masked:
            r = lax.broadcasted_iota(jnp.int32, (TQ, TK), 0)
            c = lax.broadcasted_iota(jnp.int32, (TQ, TK), 1)
            s = jnp.where(c <= r, s, -jnp.inf)
        s_ref[slot] = s

    def stage2(g, slot):
        qr = q_rows(g)
        m_prev = m_ref[qr, :]
        m_new = jnp.maximum(m_prev, jnp.max(s_ref[slot], axis=1, keepdims=True))
        a_ref[slot] = jnp.exp2(m_prev - m_new)
        mn_ref[slot] = m_new
        m_ref[qr, :] = m_new

    def stage3(g, slot):
        qr, kr = q_rows(g), k_rows(g)
        m_new = mn_ref[slot]
        p = jnp.exp2(s_ref[slot] - jnp.concatenate([m_new] * (TK // LANES), axis=1))
        vp = jnp.concatenate([v_ref[kr, :], ones], axis=1)
        pv = jnp.dot(p.astype(BF16), vp, preferred_element_type=F32)
        alpha = a_ref[slot]
        acc_ref[qr, :] = acc_ref[qr, :] * jnp.concatenate([alpha, alpha], axis=1) + pv

    def body(i, c, base, masked):
        g0 = base + ATTN_SLOTS * i
        for u in range(ATTN_SLOTS):
            g = g0 + u
            stage3(jnp.maximum(g - 2, 0), (u + ATTN_SLOTS - 2) % ATTN_SLOTS)
            stage2(jnp.maximum(g - 1, 0), (u + ATTN_SLOTS - 1) % ATTN_SLOTS)
            stage1(g, u, masked)
        return c

    assert n_off % ATTN_SLOTS == 0 and n_diag % ATTN_SLOTS == 0 and ATTN_SLOTS >= 3
    total = n_off + n_diag
    lax.fori_loop(0, n_off // ATTN_SLOTS, functools.partial(body, base=0, masked=False), 0)
    lax.fori_loop(0, n_diag // ATTN_SLOTS, functools.partial(body, base=n_off, masked=True), 0)
    stage3(total - 2, ATTN_SLOTS - 2)
    stage2(total - 1, ATTN_SLOTS - 1)
    stage3(total - 1, ATTN_SLOTS - 1)

    def fin(r, c):
        sl = pl.ds(pl.multiple_of(r * TQ, TQ), TQ)
        acc = acc_ref[sl, :]
        o_ref[sl, :] = (acc[:, :HEAD_DIM] / acc[:, HEAD_DIM:]).astype(BF16)
        return c
    lax.fori_loop(0, seq // TQ, fin, 0)


def _attention(z, qa, ka, cast_weights, *, batch, seq, n_heads):
    assert TQ == TK
    nq = seq // TQ
    t = batch * seq
    steps = batch * n_heads
    cast_specs = [pl.BlockSpec((w.shape[0] // steps, w.shape[1]),
                               lambda b, h, qt, jt: (b * n_heads + h, 0)) for w in cast_weights]
    off = [(i, j) for i in range(nq) for j in range(i)]
    diag = [(i, i) for i in range(nq)]
    assert len(diag) >= 2
    qtab = jnp.asarray([p[0] for p in off + diag], jnp.int32)
    jtab = jnp.asarray([p[1] for p in off + diag], jnp.int32)
    grid_spec = pltpu.PrefetchScalarGridSpec(
        num_scalar_prefetch=2,
        grid=(batch, n_heads),
        in_specs=[
            pl.BlockSpec((seq, HEAD_DIM), lambda b, h, qt, jt: (b, h)),
            pl.BlockSpec((seq, LANES), lambda b, h, qt, jt: (b, 0)),
            pl.BlockSpec((seq, HEAD_DIM), lambda b, h, qt, jt: (b, n_heads + h)),
            pl.BlockSpec((seq, LANES), lambda b, h, qt, jt: (b, 0)),
            pl.BlockSpec((seq, HEAD_DIM), lambda b, h, qt, jt: (b, 2 * n_heads + h)),
        ] + cast_specs,
        out_specs=[pl.BlockSpec((seq, HEAD_DIM), lambda b, h, qt, jt: (b, h))] + cast_specs,
        scratch_shapes=[
            pltpu.VMEM((seq, LANES), BF16),
            pltpu.VMEM((seq, LANES), F32),
            pltpu.VMEM((seq, 2 * HEAD_DIM), F32),
            pltpu.VMEM((ATTN_SLOTS, TQ, TK), F32),
            pltpu.VMEM((ATTN_SLOTS, TQ, LANES), F32),
            pltpu.VMEM((ATTN_SLOTS, TQ, LANES), F32),
        ],
    )
    outs = pl.pallas_call(
        functools.partial(_attn_kernel, n_off=len(off), n_diag=len(diag),
                          n_cast=len(cast_weights)),
        grid_spec=grid_spec,
        out_shape=[jax.ShapeDtypeStruct((t, n_heads * HEAD_DIM), BF16)]
        + [jax.ShapeDtypeStruct(w.shape, BF16) for w in cast_weights],
        compiler_params=_params("parallel", "parallel"),
        name="attention",
    )(qtab, jtab, z, qa, z, ka, z, *cast_weights)
    return outs[0], outs[1:]


def _gmlp_kernel(zu_ref, zv_ref, lng_ref, lnb_ref, ws_ref, bs_ref, og_ref, o_ref,
                 w_sc, v_sc, gm_sc):
    tm = zu_ref.shape[0]
    n_heads = ws_ref.shape[0]

    @pl.when(pl.program_id(0) == 0)
    def _():
        r = lax.broadcasted_iota(jnp.int32, (CHUNK, CHUNK), 0)
        c = lax.broadcasted_iota(jnp.int32, (CHUNK, CHUNK), 1)
        for h in range(n_heads):
            w_sc[h] = jnp.where(c <= r, ws_ref[h], 0.0).astype(BF16)

    for cc in range(tm // CHUNK):
        rows = slice(cc * CHUNK, (cc + 1) * CHUNK)
        gv = _gelu(zv_ref[rows, :].astype(F32))
        mu = jnp.mean(gv, axis=-1, keepdims=True)
        xc = gv - mu
        var = jnp.mean(xc * xc, axis=-1, keepdims=True)
        v_sc[rows, :] = (xc * lax.rsqrt(var + EPS) * lng_ref[...] + lnb_ref[...]).astype(BF16)

    for cc in range(tm // CHUNK):
        rows = slice(cc * CHUNK, (cc + 1) * CHUNK)
        for h in range(n_heads):
            cols = slice(h * HEAD_DIM, (h + 1) * HEAD_DIM)
            mix = jnp.dot(w_sc[h], v_sc[rows, cols], preferred_element_type=F32) + bs_ref[h]
            gm_sc[rows, cols] = _gelu(zu_ref[rows, cols].astype(F32)) * mix

    for cc in range(tm // CHUNK):
        rows = slice(cc * CHUNK, (cc + 1) * CHUNK)
        o_ref[rows, :] = _rms(gm_sc[rows, :], og_ref[...]).astype(BF16)


def _gmlp(z, ln_g, ln_b, w_s, b_s3, out_g, *, d_attn, d_gmlp):
    t = z.shape[0]
    n_heads = w_s.shape[0]
    ucol = 3 * d_attn // d_gmlp
    return pl.pallas_call(
        _gmlp_kernel,
        grid=(t // TM_GMLP,),
        in_specs=[
            pl.BlockSpec((TM_GMLP, d_gmlp), lambda i: (i, ucol)),
            pl.BlockSpec((TM_GMLP, d_gmlp), lambda i: (i, ucol + 1)),
            pl.BlockSpec((1, d_gmlp), lambda i: (0, 0)),
            pl.BlockSpec((1, d_gmlp), lambda i: (0, 0)),
            pl.BlockSpec((n_heads, CHUNK, CHUNK), lambda i: (0, 0, 0)),
            pl.BlockSpec((n_heads, CHUNK, 1), lambda i: (0, 0, 0)),
            pl.BlockSpec((1, d_gmlp), lambda i: (0, 0)),
        ],
        out_specs=pl.BlockSpec((TM_GMLP, d_gmlp), lambda i: (i, 0)),
        out_shape=jax.ShapeDtypeStruct((t, d_gmlp), BF16),
        scratch_shapes=[
            pltpu.VMEM((n_heads, CHUNK, CHUNK), BF16),
            pltpu.VMEM((TM_GMLP, d_gmlp), BF16),
            pltpu.VMEM((TM_GMLP, d_gmlp), F32),
        ],
        compiler_params=_params("arbitrary"),
        name="gmlp",
    )(z, z, ln_g, ln_b, w_s, b_s3, out_g)


def _outproj_kernel(a_ref, gm_ref, x_ref, ag_ref, w_ref, o_ref, an_ref):
    tm = a_ref.shape[0]
    d_attn = a_ref.shape[1]

    for r in range(tm // ROWS):
        sl = slice(r * ROWS, (r + 1) * ROWS)
        an_ref[sl, :] = _rms(a_ref[sl, :].astype(F32), ag_ref[...]).astype(BF16)
    y = (jnp.dot(an_ref[...], w_ref[:d_attn, :], preferred_element_type=F32)
         + jnp.dot(gm_ref[...], w_ref[d_attn:, :], preferred_element_type=F32))
    o_ref[...] = x_ref[...] + y


def _outproj(attn, gmn, x2d, attn_g, w_out):
    t, d = x2d.shape
    d_attn = attn.shape[1]
    d_gmlp = gmn.shape[1]
    return pl.pallas_call(
        _outproj_kernel,
        grid=(t // TM_OUT,),
        in_specs=[
            pl.BlockSpec((TM_OUT, d_attn), lambda i: (i, 0)),
            pl.BlockSpec((TM_OUT, d_gmlp), lambda i: (i, 0)),
            pl.BlockSpec((TM_OUT, d), lambda i: (i, 0)),
            pl.BlockSpec((1, d_attn), lambda i: (0, 0)),
            pl.BlockSpec((d_attn + d_gmlp, d), lambda i: (0, 0)),
        ],
        out_specs=pl.BlockSpec((TM_OUT, d), lambda i: (i, 0)),
        out_shape=jax.ShapeDtypeStruct((t, d), F32),
        scratch_shapes=[pltpu.VMEM((TM_OUT, d_attn), BF16)],
        compiler_params=_params("parallel"),
        name="outproj",
    )(attn, gmn, x2d, attn_g, w_out)


def _ffn_kernel(x_ref, g_ref, w1_ref, w2_ref, gf_ref, o_ref, h_ref, *, final_norm):
    f = pl.program_id(1)
    last = pl.num_programs(1) - 1
    tm = x_ref.shape[0]

    def mlp():
        a = jnp.maximum(jnp.dot(h_ref[...], w1_ref[...], preferred_element_type=F32), 0.0)
        return jnp.dot((a * a).astype(BF16), w2_ref[...], preferred_element_type=F32)

    @pl.when(f == 0)
    def _():
        for r in range(tm // ROWS):
            sl = slice(r * ROWS, (r + 1) * ROWS)
            h_ref[sl, :] = _rms(x_ref[sl, :], g_ref[...]).astype(BF16)
        o_ref[...] = mlp()

    @pl.when(jnp.logical_and(f > 0, f < last))
    def _():
        o_ref[...] += mlp()

    @pl.when(f == last)
    def _():
        y = x_ref[...] + (o_ref[...] + mlp())
        if final_norm:
            for r in range(tm // ROWS):
                sl = slice(r * ROWS, (r + 1) * ROWS)
                o_ref[sl, :] = _rms(y[sl, :], gf_ref[...])
        else:
            o_ref[...] = y


def _ffn(x2d, g, w1, w2, g_final, *, final_norm):
    t, d = x2d.shape
    d_ff = w1.shape[1]
    assert d_ff // TF_FFN >= 2
    return pl.pallas_call(
        functools.partial(_ffn_kernel, final_norm=final_norm),
        grid=(t // TM_FFN, d_ff // TF_FFN),
        in_specs=[
            pl.BlockSpec((TM_FFN, d), lambda i, f: (i, 0)),
            pl.BlockSpec((1, d), lambda i, f: (0, 0)),
            pl.BlockSpec((d, TF_FFN), lambda i, f: (0, f)),
            pl.BlockSpec((TF_FFN, d), lambda i, f: (f, 0)),
            pl.BlockSpec((1, d), lambda i, f: (0, 0)),
        ],
        out_specs=pl.BlockSpec((TM_FFN, d), lambda i, f: (i, 0)),
        out_shape=jax.ShapeDtypeStruct((t, d), F32),
        scratch_shapes=[pltpu.VMEM((TM_FFN, d), BF16)],
        compiler_params=_params("parallel", "arbitrary"),
        name="ffn",
    )(x2d, g, w1, w2, g_final)


def kernel(x, norm_mix_g, w_in, b_f, gmlp_ln_g, gmlp_ln_b, w_s, b_s, attn_out_g, gmlp_out_g,
           w_out, norm_ffn_g, w_ff1, w_ff2, norm_final_g):
    batch, seq, d = x.shape
    depth = w_in.shape[0]
    n_heads = b_f.shape[1]
    d_attn = n_heads * HEAD_DIM
    d_gmlp = gmlp_ln_g.shape[1]
    o3 = 3 * d_attn
    o4 = o3 + n_heads
    assert d_attn == d_gmlp and w_in.shape[2] == o4 + 2 * d_gmlp

    x2d = x.reshape(batch * seq, d)
    for l in range(depth):
        w_main, w_gate = _wprep(w_in[l], o3=o3, o4=o4)
        w_f = _decay_lanes(w_gate[:, :n_heads], n_heads)
        bf_row = _decay_lanes(b_f[l].astype(F32), n_heads).reshape(1, LANES)

        z, zf = _inproj(x2d, norm_mix_g[l].reshape(1, d), w_main, w_f, d_attn=d_attn)
        qa, ka = _decay(zf, bf_row, batch=batch, seq=seq, n_heads=n_heads)
        attn, (w_out16, w_ff1_16, w_ff2_16) = _attention(
            z, qa, ka, [w_out[l], w_ff1[l], w_ff2[l]], batch=batch, seq=seq, n_heads=n_heads)
        gmn = _gmlp(z, gmlp_ln_g[l].reshape(1, d_gmlp), gmlp_ln_b[l].reshape(1, d_gmlp),
                    w_s[l], b_s[l].reshape(n_heads, CHUNK, 1), gmlp_out_g[l].reshape(1, d_gmlp),
                    d_attn=d_attn, d_gmlp=d_gmlp)
        x2d = _outproj(attn, gmn, x2d, attn_out_g[l].reshape(1, d_attn), w_out16)
        x2d = _ffn(x2d, norm_ffn_g[l].reshape(1, d), w_ff1_16, w_ff2_16,
                   norm_final_g.reshape(1, d), final_norm=(l == depth - 1))
    return x2d.reshape(batch, seq, d)
```

```python
import functools
import math

import numpy as np
import jax
import jax.numpy as jnp
from jax import lax
from jax.experimental import pallas as pl
from jax.experimental.pallas import tpu as pltpu

F32 = jnp.float32
BF16 = jnp.bfloat16

HEAD_DIM = 128
CHUNK = 128
EPS = 1e-6
LANES = 128
LOG2E = 1.4426950408889634
VMEM_LIMIT = 62 * 1024 * 1024

TM_WPREP = 256
TM_PROJ = 1024
TN_PROJ = 1024
TM_DECAY = 1024
TQ = 512
TK = 512
ATTN_SLOTS = 4
TM_GMLP = 512
TM_OUT = 512
TM_FFN = 1024
TF_FFN = 1024
ROWS = 128
CUM_BLK = 256
DECAY_LANES = 8


def _rms(x, g):
    ms = jnp.mean(x * x, axis=-1, keepdims=True)
    return x * lax.rsqrt(ms + EPS) * g


def _gelu(x):
    c = math.sqrt(2.0 / math.pi)
    return x * (0.5 * (1.0 + jnp.tanh(c * (x + 0.044715 * (x * x * x)))))


def _split3(x):
    hi = x.astype(BF16)
    r1 = x - hi.astype(F32)
    mid = r1.astype(BF16)
    lo = (r1 - mid.astype(F32)).astype(BF16)
    return hi, mid, lo


def _params(*sem):
    return pltpu.CompilerParams(dimension_semantics=sem, vmem_limit_bytes=VMEM_LIMIT)


def _wprep_kernel(a_ref, b_ref, main_ref, gate_ref, *, gate_block, n_gate):
    j = pl.program_id(0)

    @pl.when(j < gate_block)
    def _():
        main_ref[...] = a_ref[...].astype(BF16)

    @pl.when(j >= gate_block)
    def _():
        main_ref[...] = jnp.concatenate([a_ref[n_gate:, :], b_ref[...]], axis=0).astype(BF16)

    @pl.when(j == gate_block)
    def _():
        gate_ref[...] = a_ref[:n_gate, :]


def _wprep(wt, *, o3, n_gate):
    n, d = wt.shape
    assert o3 % TM_WPREP == 0 and (n - n_gate) % TM_WPREP == 0 and TM_WPREP % n_gate == 0
    assert n_gate % 8 == 0
    per = TM_WPREP // n_gate
    return pl.pallas_call(
        functools.partial(_wprep_kernel, gate_block=o3 // TM_WPREP, n_gate=n_gate),
        grid=((n - n_gate) // TM_WPREP,),
        in_specs=[
            pl.BlockSpec((TM_WPREP, d), lambda j: (j, 0)),
            pl.BlockSpec((n_gate, d), lambda j: ((j + 1) * per, 0)),
        ],
        out_specs=[
            pl.BlockSpec((TM_WPREP, d), lambda j: (j, 0)),
            pl.BlockSpec((n_gate, d), lambda j: (0, 0)),
        ],
        out_shape=[
            jax.ShapeDtypeStruct((n - n_gate, d), BF16),
            jax.ShapeDtypeStruct((n_gate, d), F32),
        ],
        compiler_params=_params("arbitrary"),
        name="wprep",
    )(wt, wt)


def _inproj_kernel(x_ref, g_ref, w_ref, wf_ref, z_ref, zf_ref, h_ref, *, n_q_tiles, q_scale):
    j = pl.program_id(1)
    tm = x_ref.shape[0]

    nt = (((1,), (1,)), ((), ()))

    def project():
        acc = lax.dot_general(h_ref[...], w_ref[...], nt, preferred_element_type=F32)
        scale = jnp.where(j < n_q_tiles, q_scale, 1.0).astype(F32)
        z_ref[...] = (acc * scale).astype(BF16)

    @pl.when(j == 0)
    def _():
        for r in range(tm // ROWS):
            sl = slice(r * ROWS, (r + 1) * ROWS)
            h_ref[sl, :] = _rms(x_ref[sl, :], g_ref[...]).astype(BF16)
        zf_ref[...] = lax.dot_general(h_ref[...], wf_ref[...], nt, preferred_element_type=F32)
        project()

    @pl.when(j > 0)
    def _():
        project()


def _inproj(x2d, g, w_main, w_f, *, d_attn):
    t, d = x2d.shape
    n = w_main.shape[0]
    q_scale = LOG2E / math.sqrt(HEAD_DIM)
    return pl.pallas_call(
        functools.partial(_inproj_kernel, n_q_tiles=d_attn // TN_PROJ, q_scale=q_scale),
        grid=(t // TM_PROJ, n // TN_PROJ),
        in_specs=[
            pl.BlockSpec((TM_PROJ, d), lambda i, j: (i, 0)),
            pl.BlockSpec((1, d), lambda i, j: (0, 0)),
            pl.BlockSpec((TN_PROJ, d), lambda i, j: (j, 0)),
            pl.BlockSpec((LANES, d), lambda i, j: (0, 0)),
        ],
        out_specs=[
            pl.BlockSpec((TM_PROJ, TN_PROJ), lambda i, j: (i, j)),
            pl.BlockSpec((TM_PROJ, LANES), lambda i, j: (i, 0)),
        ],
        out_shape=[
            jax.ShapeDtypeStruct((t, n), BF16),
            jax.ShapeDtypeStruct((t, LANES), F32),
        ],
        scratch_shapes=[pltpu.VMEM((TM_PROJ, d), BF16)],
        compiler_params=_params("parallel", "arbitrary"),
        name="inproj",
    )(x2d, g, w_main, w_f)


def _decay_kernel(zf_ref, bf_ref, qa_ref, ka_ref, f_ref, carry_ref, *, n_heads):
    rows = zf_ref.shape[0]

    @pl.when(pl.program_id(1) == 0)
    def _():
        carry_ref[...] = jnp.zeros(carry_ref.shape, F32)

    row = lax.broadcasted_iota(jnp.int32, (CUM_BLK, CUM_BLK), 0)
    col = lax.broadcasted_iota(jnp.int32, (CUM_BLK, CUM_BLK), 1)
    tri = (col <= row).astype(BF16)
    carry = carry_ref[...]
    for r in range(rows // CUM_BLK):
        sl = slice(r * CUM_BLK, (r + 1) * CUM_BLK)
        x = zf_ref[sl, :] + bf_ref[...]
        logf = jnp.minimum(x, 0.0) - jnp.log1p(jnp.exp(-jnp.abs(x)))
        hi, mid, lo = _split3(logf)
        c = (jnp.dot(tri, hi, preferred_element_type=F32)
             + jnp.dot(tri, mid, preferred_element_type=F32)
             + jnp.dot(tri, lo, preferred_element_type=F32)) + carry
        f_ref[sl, :] = c
        carry = c[CUM_BLK - 1:CUM_BLK, :]
    carry_ref[...] = carry

    hi, mid, lo = _split3(f_ref[...] * LOG2E)
    lane = lax.broadcasted_iota(jnp.int32, (rows, LANES), 1)
    sub = lane & (DECAY_LANES - 1)
    used = lane < n_heads * DECAY_LANES
    piece_idx = jnp.where(sub >= 3, sub - 3, sub)
    piece = jnp.where(piece_idx == 0, hi, jnp.where(piece_idx == 1, mid, lo)).astype(F32)
    first = jnp.logical_and(used, sub < 3)
    second = jnp.logical_and(used, jnp.logical_and(sub >= 3, sub < 6))
    qa_ref[...] = jnp.where(first, piece, jnp.where(second, 1.0, 0.0)).astype(BF16)
    ka_ref[...] = jnp.where(first, 1.0, jnp.where(second, -piece, 0.0)).astype(BF16)


def _decay_lanes(v, n_heads):
    lead = v.shape[:-1]
    rep = jnp.broadcast_to(v[..., None], lead + (n_heads, 6))
    rep = jnp.pad(rep, [(0, 0)] * len(lead) + [(0, 0), (0, DECAY_LANES - 6)])
    rep = rep.reshape(lead + (n_heads * DECAY_LANES,))
    return jnp.pad(rep, [(0, 0)] * len(lead) + [(0, LANES - n_heads * DECAY_LANES)])


def _decay(zf, bf_row, *, batch, seq, n_heads):
    assert n_heads * DECAY_LANES <= LANES
    ns = seq // TM_DECAY
    return pl.pallas_call(
        functools.partial(_decay_kernel, n_heads=n_heads),
        grid=(batch, ns),
        in_specs=[
            pl.BlockSpec((TM_DECAY, LANES), lambda b, s: (b * ns + s, 0)),
            pl.BlockSpec((1, LANES), lambda b, s: (0, 0)),
        ],
        out_specs=[
            pl.BlockSpec((TM_DECAY, LANES), lambda b, s: (b * ns + s, 0)),
            pl.BlockSpec((TM_DECAY, LANES), lambda b, s: (b * ns + s, 0)),
        ],
        out_shape=[
            jax.ShapeDtypeStruct((batch * seq, LANES), BF16),
            jax.ShapeDtypeStruct((batch * seq, LANES), BF16),
        ],
        scratch_shapes=[pltpu.VMEM((TM_DECAY, LANES), F32), pltpu.VMEM((1, LANES), F32)],
        compiler_params=_params("parallel", "arbitrary"),
        name="decay",
    )(zf, bf_row)


def _attn_kernel(qtab, jtab, q_ref, qa_ref, k_ref, ka_ref, v_ref, *rest, n_off, n_diag, n_cast):
    cast_in, rest = rest[:n_cast], rest[n_cast:]
    o_ref, rest = rest[0], rest[1:]
    cast_out, rest = rest[:n_cast], rest[n_cast:]
    for src, dst in zip(cast_in, cast_out):
        dst[...] = src[...].astype(BF16)
    _attn_body(qtab, jtab, q_ref, qa_ref, k_ref, ka_ref, v_ref, o_ref, *rest,
               n_off=n_off, n_diag=n_diag)


def _attn_body(qtab, jtab, q_ref, qa_ref, k_ref, ka_ref, v_ref, o_ref,
               qam_ref, m_ref, acc_ref, s_ref, mn_ref, a_ref, *, n_off, n_diag):
    seq = q_ref.shape[0]
    lane = lax.broadcasted_iota(jnp.int32, qa_ref.shape, 1)
    head_lane0 = pl.program_id(1) * DECAY_LANES
    mine = jnp.logical_and(lane >= head_lane0, lane < head_lane0 + DECAY_LANES)
    qam_ref[...] = jnp.where(mine, qa_ref[...], jnp.zeros(qa_ref.shape, BF16))
    m_ref[...] = jnp.full(m_ref.shape, -1e30, F32)
    acc_ref[...] = jnp.zeros(acc_ref.shape, F32)
    for slot in (ATTN_SLOTS - 2, ATTN_SLOTS - 1):
        s_ref[slot] = jnp.full((TQ, TK), -jnp.inf, F32)
        mn_ref[slot] = jnp.zeros((TQ, LANES), F32)
        a_ref[slot] = jnp.ones((TQ, LANES), F32)
    ones = jnp.ones((TK, HEAD_DIM), BF16)

    def q_rows(g):
        return pl.ds(pl.multiple_of(qtab[g] * TQ, TQ), TQ)

    def k_rows(g):
        return pl.ds(pl.multiple_of(jtab[g] * TK, TK), TK)

    def stage1(g, slot, masked):
        qr, kr = q_rows(g), k_rows(g)
        qp = jnp.concatenate([q_ref[qr, :], qam_ref[qr, :]], axis=1)
        kp = jnp.concatenate([k_ref[kr, :], ka_ref[kr, :]], axis=1)
        s = lax.dot_general(qp, kp, (((1,), (1,)), ((), ())), preferred_element_type=F32)
        if masked:
            r = lax.broadcasted_iota(jnp.int32, (TQ, TK), 0)
            c = lax.broadcasted_iota(jnp.int32, (TQ, TK), 1)
            s = jnp.where(c <= r, s, -jnp.inf)
        s_ref[slot] = s

    def stage2(g, slot):
        qr = q_rows(g)
        m_prev = m_ref[qr, :]
        m_new = jnp.maximum(m_prev, jnp.max(s_ref[slot], axis=1, keepdims=True))
        a_ref[slot] = jnp.exp2(m_prev - m_new)
        mn_ref[slot] = m_new
        m_ref[qr, :] = m_new

    def stage3(g, slot):
        qr, kr = q_rows(g), k_rows(g)
        m_new = mn_ref[slot]
        p = jnp.exp2(s_ref[slot] - jnp.concatenate([m_new] * (TK // LANES), axis=1))
        vp = jnp.concatenate([v_ref[kr, :], ones], axis=1)
        pv = jnp.dot(p.astype(BF16), vp, preferred_element_type=F32)
        alpha = a_ref[slot]
        acc_ref[qr, :] = acc_ref[qr, :] * jnp.concatenate([alpha, alpha], axis=1) + pv

    def body(i, c, base, masked):
        g0 = base + ATTN_SLOTS * i
        for u in range(ATTN_SLOTS):
            g = g0 + u
            stage3(jnp.maximum(g - 2, 0), (u + ATTN_SLOTS - 2) % ATTN_SLOTS)
            stage2(jnp.maximum(g - 1, 0), (u + ATTN_SLOTS - 1) % ATTN_SLOTS)
            stage1(g, u, masked)
        return c

    assert n_off % ATTN_SLOTS == 0 and n_diag % ATTN_SLOTS == 0 and ATTN_SLOTS >= 3
    total = n_off + n_diag
    lax.fori_loop(0, n_off // ATTN_SLOTS, functools.partial(body, base=0, masked=False), 0)
    lax.fori_loop(0, n_diag // ATTN_SLOTS, functools.partial(body, base=n_off, masked=True), 0)
    stage3(total - 2, ATTN_SLOTS - 2)
    stage2(total - 1, ATTN_SLOTS - 1)
    stage3(total - 1, ATTN_SLOTS - 1)

    def fin(r, c):
        sl = pl.ds(pl.multiple_of(r * TQ, TQ), TQ)
        acc = acc_ref[sl, :]
        o_ref[sl, :] = (acc[:, :HEAD_DIM] / acc[:, HEAD_DIM:]).astype(BF16)
        return c
    lax.fori_loop(0, seq // TQ, fin, 0)


def _attention(z, qa, ka, cast_weights, *, batch, seq, n_heads):
    assert TQ == TK
    nq = seq // TQ
    t = batch * seq
    steps = batch * n_heads
    cast_specs = [pl.BlockSpec((w.shape[0] // steps, w.shape[1]),
                               lambda b, h, qt, jt: (b * n_heads + h, 0)) for w in cast_weights]
    off = [(i, j) for i in range(nq) for j in range(i)]
    diag = [(i, i) for i in range(nq)]
    assert len(diag) >= 2
    qtab = jnp.asarray([p[0] for p in off + diag], jnp.int32)
    jtab = jnp.asarray([p[1] for p in off + diag], jnp.int32)
    grid_spec = pltpu.PrefetchScalarGridSpec(
        num_scalar_prefetch=2,
        grid=(batch, n_heads),
        in_specs=[
            pl.BlockSpec((seq, HEAD_DIM), lambda b, h, qt, jt: (b, h)),
            pl.BlockSpec((seq, LANES), lambda b, h, qt, jt: (b, 0)),
            pl.BlockSpec((seq, HEAD_DIM), lambda b, h, qt, jt: (b, n_heads + h)),
            pl.BlockSpec((seq, LANES), lambda b, h, qt, jt: (b, 0)),
            pl.BlockSpec((seq, HEAD_DIM), lambda b, h, qt, jt: (b, 2 * n_heads + h)),
        ] + cast_specs,
        out_specs=[pl.BlockSpec((seq, HEAD_DIM), lambda b, h, qt, jt: (b, h))] + cast_specs,
        scratch_shapes=[
            pltpu.VMEM((seq, LANES), BF16),
            pltpu.VMEM((seq, LANES), F32),
            pltpu.VMEM((seq, 2 * HEAD_DIM), F32),
            pltpu.VMEM((ATTN_SLOTS, TQ, TK), F32),
            pltpu.VMEM((ATTN_SLOTS, TQ, LANES), F32),
            pltpu.VMEM((ATTN_SLOTS, TQ, LANES), F32),
        ],
    )
    outs = pl.pallas_call(
        functools.partial(_attn_kernel, n_off=len(off), n_diag=len(diag),
                          n_cast=len(cast_weights)),
        grid_spec=grid_spec,
        out_shape=[jax.ShapeDtypeStruct((t, n_heads * HEAD_DIM), BF16)]
        + [jax.ShapeDtypeStruct(w.shape, BF16) for w in cast_weights],
        compiler_params=_params("parallel", "parallel"),
        name="attention",
    )(qtab, jtab, z, qa, z, ka, z, *cast_weights)
    return outs[0], outs[1:]


def _gmlp_kernel(zu_ref, zv_ref, lng_ref, lnb_ref, ws_ref, bs_ref, og_ref, o_ref,
                 w_sc, v_sc, gm_sc):
    tm = zu_ref.shape[0]
    n_heads = ws_ref.shape[0]

    @pl.when(pl.program_id(0) == 0)
    def _():
        r = lax.broadcasted_iota(jnp.int32, (CHUNK, CHUNK), 0)
        c = lax.broadcasted_iota(jnp.int32, (CHUNK, CHUNK), 1)
        for h in range(n_heads):
            w_sc[h] = jnp.where(c <= r, ws_ref[h], 0.0).astype(BF16)

    for cc in range(tm // CHUNK):
        rows = slice(cc * CHUNK, (cc + 1) * CHUNK)
        gv = _gelu(zv_ref[rows, :].astype(F32))
        mu = jnp.mean(gv, axis=-1, keepdims=True)
        xc = gv - mu
        var = jnp.mean(xc * xc, axis=-1, keepdims=True)
        v_sc[rows, :] = (xc * lax.rsqrt(var + EPS) * lng_ref[...] + lnb_ref[...]).astype(BF16)

    for cc in range(tm // CHUNK):
        rows = slice(cc * CHUNK, (cc + 1) * CHUNK)
        for h in range(n_heads):
            cols = slice(h * HEAD_DIM, (h + 1) * HEAD_DIM)
            mix = jnp.dot(w_sc[h], v_sc[rows, cols], preferred_element_type=F32) + bs_ref[h]
            gm_sc[rows, cols] = _gelu(zu_ref[rows, cols].astype(F32)) * mix

    for cc in range(tm // CHUNK):
        rows = slice(cc * CHUNK, (cc + 1) * CHUNK)
        o_ref[rows, :] = _rms(gm_sc[rows, :], og_ref[...]).astype(BF16)


def _gmlp(z, ln_g, ln_b, w_s, b_s3, out_g, *, d_attn, d_gmlp):
    t = z.shape[0]
    n_heads = w_s.shape[0]
    ucol = 3 * d_attn // d_gmlp
    return pl.pallas_call(
        _gmlp_kernel,
        grid=(t // TM_GMLP,),
        in_specs=[
            pl.BlockSpec((TM_GMLP, d_gmlp), lambda i: (i, ucol)),
            pl.BlockSpec((TM_GMLP, d_gmlp), lambda i: (i, ucol + 1)),
            pl.BlockSpec((1, d_gmlp), lambda i: (0, 0)),
            pl.BlockSpec((1, d_gmlp), lambda i: (0, 0)),
            pl.BlockSpec((n_heads, CHUNK, CHUNK), lambda i: (0, 0, 0)),
            pl.BlockSpec((n_heads, CHUNK, 1), lambda i: (0, 0, 0)),
            pl.BlockSpec((1, d_gmlp), lambda i: (0, 0)),
        ],
        out_specs=pl.BlockSpec((TM_GMLP, d_gmlp), lambda i: (i, 0)),
        out_shape=jax.ShapeDtypeStruct((t, d_gmlp), BF16),
        scratch_shapes=[
            pltpu.VMEM((n_heads, CHUNK, CHUNK), BF16),
            pltpu.VMEM((TM_GMLP, d_gmlp), BF16),
            pltpu.VMEM((TM_GMLP, d_gmlp), F32),
        ],
        compiler_params=_params("arbitrary"),
        name="gmlp",
    )(z, z, ln_g, ln_b, w_s, b_s3, out_g)


def _outproj_kernel(a_ref, gm_ref, x_ref, ag_ref, w_ref, o_ref, an_ref):
    tm = a_ref.shape[0]
    d_attn = a_ref.shape[1]

    for r in range(tm // ROWS):
        sl = slice(r * ROWS, (r + 1) * ROWS)
        an_ref[sl, :] = _rms(a_ref[sl, :].astype(F32), ag_ref[...]).astype(BF16)
    y = (jnp.dot(an_ref[...], w_ref[:d_attn, :], preferred_element_type=F32)
         + jnp.dot(gm_ref[...], w_ref[d_attn:, :], preferred_element_type=F32))
    o_ref[...] = x_ref[...] + y


def _outproj(attn, gmn, x2d, attn_g, w_out):
    t, d = x2d.shape
    d_attn = attn.shape[1]
    d_gmlp = gmn.shape[1]
    return pl.pallas_call(
        _outproj_kernel,
        grid=(t // TM_OUT,),
        in_specs=[
            pl.BlockSpec((TM_OUT, d_attn), lambda i: (i, 0)),
            pl.BlockSpec((TM_OUT, d_gmlp), lambda i: (i, 0)),
            pl.BlockSpec((TM_OUT, d), lambda i: (i, 0)),
            pl.BlockSpec((1, d_attn), lambda i: (0, 0)),
            pl.BlockSpec((d_attn + d_gmlp, d), lambda i: (0, 0)),
        ],
        out_specs=pl.BlockSpec((TM_OUT, d), lambda i: (i, 0)),
        out_shape=jax.ShapeDtypeStruct((t, d), F32),
        scratch_shapes=[pltpu.VMEM((TM_OUT, d_attn), BF16)],
        compiler_params=_params("parallel"),
        name="outproj",
    )(attn, gmn, x2d, attn_g, w_out)


def _ffn_kernel(x_ref, g_ref, w1_ref, w2_ref, gf_ref, o_ref, h_ref, *, final_norm):
    f = pl.program_id(1)
    last = pl.num_programs(1) - 1
    tm = x_ref.shape[0]

    def mlp():
        a = jnp.maximum(jnp.dot(h_ref[...], w1_ref[...], preferred_element_type=F32), 0.0)
        return jnp.dot((a * a).astype(BF16), w2_ref[...], preferred_element_type=F32)

    @pl.when(f == 0)
    def _():
        for r in range(tm // ROWS):
            sl = slice(r * ROWS, (r + 1) * ROWS)
            h_ref[sl, :] = _rms(x_ref[sl, :], g_ref[...]).astype(BF16)
        o_ref[...] = mlp()

    @pl.when(jnp.logical_and(f > 0, f < last))
    def _():
        o_ref[...] += mlp()

    @pl.when(f == last)
    def _():
        y = x_ref[...] + (o_ref[...] + mlp())
        if final_norm:
            for r in range(tm // ROWS):
                sl = slice(r * ROWS, (r + 1) * ROWS)
                o_ref[sl, :] = _rms(y[sl, :], gf_ref[...])
        else:
            o_ref[...] = y


def _ffn(x2d, g, w1, w2, g_final, *, final_norm):
    t, d = x2d.shape
    d_ff = w1.shape[1]
    assert d_ff // TF_FFN >= 2
    return pl.pallas_call(
        functools.partial(_ffn_kernel, final_norm=final_norm),
        grid=(t // TM_FFN, d_ff // TF_FFN),
        in_specs=[
            pl.BlockSpec((TM_FFN, d), lambda i, f: (i, 0)),
            pl.BlockSpec((1, d), lambda i, f: (0, 0)),
            pl.BlockSpec((d, TF_FFN), lambda i, f: (0, f)),
            pl.BlockSpec((TF_FFN, d), lambda i, f: (f, 0)),
            pl.BlockSpec((1, d), lambda i, f: (0, 0)),
        ],
        out_specs=pl.BlockSpec((TM_FFN, d), lambda i, f: (i, 0)),
        out_shape=jax.ShapeDtypeStruct((t, d), F32),
        scratch_shapes=[pltpu.VMEM((TM_FFN, d), BF16)],
        compiler_params=_params("parallel", "arbitrary"),
        name="ffn",
    )(x2d, g, w1, w2, g_final)


def kernel(x, norm_mix_g, w_in, b_f, gmlp_ln_g, gmlp_ln_b, w_s, b_s, attn_out_g, gmlp_out_g,
           w_out, norm_ffn_g, w_ff1, w_ff2, norm_final_g):
    batch, seq, d = x.shape
    depth = w_in.shape[0]
    n_heads = b_f.shape[1]
    d_attn = n_heads * HEAD_DIM
    d_gmlp = gmlp_ln_g.shape[1]
    o3 = 3 * d_attn
    o4 = o3 + n_heads
    assert d_attn == d_gmlp and w_in.shape[2] == o4 + 2 * d_gmlp

    x2d = x.reshape(batch * seq, d)
    for l in range(depth):
        w_main, w_gate = _wprep(jnp.swapaxes(w_in[l], 0, 1), o3=o3, n_gate=n_heads)
        w_f = jnp.swapaxes(_decay_lanes(jnp.swapaxes(w_gate, 0, 1), n_heads), 0, 1).astype(BF16)
        bf_row = _decay_lanes(b_f[l].astype(F32), n_heads).reshape(1, LANES)

        z, zf = _inproj(x2d, norm_mix_g[l].reshape(1, d), w_main, w_f, d_attn=d_attn)
        qa, ka = _decay(zf, bf_row, batch=batch, seq=seq, n_heads=n_heads)
        attn, (w_out16, w_ff1_16, w_ff2_16) = _attention(
            z, qa, ka, [w_out[l], w_ff1[l], w_ff2[l]], batch=batch, seq=seq, n_heads=n_heads)
        gmn = _gmlp(z, gmlp_ln_g[l].reshape(1, d_gmlp), gmlp_ln_b[l].reshape(1, d_gmlp),
                    w_s[l], b_s[l].reshape(n_heads, CHUNK, 1), gmlp_out_g[l].reshape(1, d_gmlp),
                    d_attn=d_attn, d_gmlp=d_gmlp)
        x2d = _outproj(attn, gmn, x2d, attn_out_g[l].reshape(1, d_attn), w_out16)
        x2d = _ffn(x2d, norm_ffn_g[l].reshape(1, d), w_ff1_16, w_ff2_16,
                   norm_final_g.reshape(1, d), final_norm=(l == depth - 1))
    return x2d.reshape(batch, seq, d)
```

```python
import functools
import math

import numpy as np
import jax
import jax.numpy as jnp
from jax import lax
from jax.experimental import pallas as pl
from jax.experimental.pallas import tpu as pltpu

F32 = jnp.float32
BF16 = jnp.bfloat16

HEAD_DIM = 128
CHUNK = 128
EPS = 1e-6
LANES = 128
LOG2E = 1.4426950408889634
VMEM_LIMIT = 62 * 1024 * 1024

TM_WPREP = 256
TM_PROJ = 1024
TN_PROJ = 2560
TM_DECAY = 1024
TQ = 512
TK = 512
ATTN_SLOTS = 4
TM_OUT = 512
TM_FFN = 1024
TF_FFN = 1024
ROWS = 128
CUM_BLK = 256
DECAY_LANES = 8


def _rms(x, g):
    ms = jnp.mean(x * x, axis=-1, keepdims=True)
    return x * lax.rsqrt(ms + EPS) * g


def _gelu(x):
    c = math.sqrt(2.0 / math.pi)
    inner = x * (c + (c * 0.044715) * (x * x))
    hx = 0.5 * x
    return hx + hx * jnp.tanh(inner)


def _split3(x):
    hi = x.astype(BF16)
    r1 = x - hi.astype(F32)
    mid = r1.astype(BF16)
    lo = (r1 - mid.astype(F32)).astype(BF16)
    return hi, mid, lo


def _params(*sem):
    return pltpu.CompilerParams(dimension_semantics=sem, vmem_limit_bytes=VMEM_LIMIT)


def _wprep_kernel(a_ref, b_ref, main_ref, gate_ref, *, gate_block, n_gate):
    j = pl.program_id(0)

    @pl.when(j < gate_block)
    def _():
        main_ref[...] = a_ref[...].astype(BF16)

    @pl.when(j >= gate_block)
    def _():
        main_ref[...] = jnp.concatenate([a_ref[n_gate:, :], b_ref[...]], axis=0).astype(BF16)

    @pl.when(j == gate_block)
    def _():
        gate_ref[...] = a_ref[:n_gate, :]


def _wprep(wt, *, o3, n_gate):
    n, d = wt.shape
    assert o3 % TM_WPREP == 0 and (n - n_gate) % TM_WPREP == 0 and TM_WPREP % n_gate == 0
    assert n_gate % 8 == 0
    per = TM_WPREP // n_gate
    return pl.pallas_call(
        functools.partial(_wprep_kernel, gate_block=o3 // TM_WPREP, n_gate=n_gate),
        grid=((n - n_gate) // TM_WPREP,),
        in_specs=[
            pl.BlockSpec((TM_WPREP, d), lambda j: (j, 0)),
            pl.BlockSpec((n_gate, d), lambda j: ((j + 1) * per, 0)),
        ],
        out_specs=[
            pl.BlockSpec((TM_WPREP, d), lambda j: (j, 0)),
            pl.BlockSpec((n_gate, d), lambda j: (0, 0)),
        ],
        out_shape=[
            jax.ShapeDtypeStruct((n - n_gate, d), BF16),
            jax.ShapeDtypeStruct((n_gate, d), F32),
        ],
        compiler_params=_params("arbitrary"),
        name="wprep",
    )(wt, wt)


def _inproj_kernel(x_ref, g_ref, w_ref, wf_ref, z_ref, zf_ref, h_ref, *, d_q, q_scale):
    j = pl.program_id(1)
    tm = x_ref.shape[0]

    nt = (((1,), (1,)), ((), ()))

    @pl.when(j == 0)
    def _():
        for r in range(tm // ROWS):
            sl = slice(r * ROWS, (r + 1) * ROWS)
            h_ref[sl, :] = _rms(x_ref[sl, :], g_ref[...]).astype(BF16)
        zf_ref[...] = lax.dot_general(h_ref[...], wf_ref[...], nt, preferred_element_type=F32)
        q = lax.dot_general(h_ref[...], w_ref[:d_q, :], nt, preferred_element_type=F32)
        z_ref[:, :d_q] = (q * q_scale).astype(BF16)
        z_ref[:, d_q:] = lax.dot_general(h_ref[...], w_ref[d_q:, :], nt,
                                         preferred_element_type=F32).astype(BF16)

    @pl.when(j > 0)
    def _():
        z_ref[...] = lax.dot_general(h_ref[...], w_ref[...], nt,
                                     preferred_element_type=F32).astype(BF16)


def _inproj(x2d, g, w_main, w_f, *, d_attn):
    t, d = x2d.shape
    n = w_main.shape[0]
    q_scale = LOG2E / math.sqrt(HEAD_DIM)
    assert d_attn < TN_PROJ and d_attn % LANES == 0
    return pl.pallas_call(
        functools.partial(_inproj_kernel, d_q=d_attn, q_scale=q_scale),
        grid=(t // TM_PROJ, n // TN_PROJ),
        in_specs=[
            pl.BlockSpec((TM_PROJ, d), lambda i, j: (i, 0)),
            pl.BlockSpec((1, d), lambda i, j: (0, 0)),
            pl.BlockSpec((TN_PROJ, d), lambda i, j: (j, 0)),
            pl.BlockSpec((LANES, d), lambda i, j: (0, 0)),
        ],
        out_specs=[
            pl.BlockSpec((TM_PROJ, TN_PROJ), lambda i, j: (i, j)),
            pl.BlockSpec((TM_PROJ, LANES), lambda i, j: (i, 0)),
        ],
        out_shape=[
            jax.ShapeDtypeStruct((t, n), BF16),
            jax.ShapeDtypeStruct((t, LANES), F32),
        ],
        scratch_shapes=[pltpu.VMEM((TM_PROJ, d), BF16)],
        compiler_params=_params("parallel", "arbitrary"),
        name="inproj",
    )(x2d, g, w_main, w_f)


def _decay_kernel(zf_ref, bf_ref, qa_ref, ka_ref, f_ref, carry_ref, *, n_heads):
    rows = zf_ref.shape[0]

    @pl.when(pl.program_id(1) == 0)
    def _():
        carry_ref[...] = jnp.zeros(carry_ref.shape, F32)

    row = lax.broadcasted_iota(jnp.int32, (CUM_BLK, CUM_BLK), 0)
    col = lax.broadcasted_iota(jnp.int32, (CUM_BLK, CUM_BLK), 1)
    tri = (col <= row).astype(BF16)
    carry = carry_ref[...]
    for r in range(rows // CUM_BLK):
        sl = slice(r * CUM_BLK, (r + 1) * CUM_BLK)
        x = zf_ref[sl, :] + bf_ref[...]
        logf = jnp.minimum(x, 0.0) - jnp.log1p(jnp.exp(-jnp.abs(x)))
        hi, mid, lo = _split3(logf)
        c = (jnp.dot(tri, hi, preferred_element_type=F32)
             + jnp.dot(tri, mid, preferred_element_type=F32)
             + jnp.dot(tri, lo, preferred_element_type=F32)) + carry
        f_ref[sl, :] = c
        carry = c[CUM_BLK - 1:CUM_BLK, :]
    carry_ref[...] = carry

    hi, mid, lo = _split3(f_ref[...] * LOG2E)
    lane = lax.broadcasted_iota(jnp.int32, (rows, LANES), 1)
    sub = lane & (DECAY_LANES - 1)
    used = lane < n_heads * DECAY_LANES
    piece_idx = jnp.where(sub >= 3, sub - 3, sub)
    piece = jnp.where(piece_idx == 0, hi, jnp.where(piece_idx == 1, mid, lo)).astype(F32)
    first = jnp.logical_and(used, sub < 3)
    second = jnp.logical_and(used, jnp.logical_and(sub >= 3, sub < 6))
    qa_ref[...] = jnp.where(first, piece, jnp.where(second, 1.0, 0.0)).astype(BF16)
    ka_ref[...] = jnp.where(first, 1.0, jnp.where(second, -piece, 0.0)).astype(BF16)


def _decay_lanes(v, n_heads):
    lead = v.shape[:-1]
    rep = jnp.broadcast_to(v[..., None], lead + (n_heads, 6))
    rep = jnp.pad(rep, [(0, 0)] * len(lead) + [(0, 0), (0, DECAY_LANES - 6)])
    rep = rep.reshape(lead + (n_heads * DECAY_LANES,))
    return jnp.pad(rep, [(0, 0)] * len(lead) + [(0, LANES - n_heads * DECAY_LANES)])


def _decay(zf, bf_row, *, batch, seq, n_heads):
    assert n_heads * DECAY_LANES <= LANES
    ns = seq // TM_DECAY
    return pl.pallas_call(
        functools.partial(_decay_kernel, n_heads=n_heads),
        grid=(batch, ns),
        in_specs=[
            pl.BlockSpec((TM_DECAY, LANES), lambda b, s: (b * ns + s, 0)),
            pl.BlockSpec((1, LANES), lambda b, s: (0, 0)),
        ],
        out_specs=[
            pl.BlockSpec((TM_DECAY, LANES), lambda b, s: (b * ns + s, 0)),
            pl.BlockSpec((TM_DECAY, LANES), lambda b, s: (b * ns + s, 0)),
        ],
        out_shape=[
            jax.ShapeDtypeStruct((batch * seq, LANES), BF16),
            jax.ShapeDtypeStruct((batch * seq, LANES), BF16),
        ],
        scratch_shapes=[pltpu.VMEM((TM_DECAY, LANES), F32), pltpu.VMEM((1, LANES), F32)],
        compiler_params=_params("parallel", "arbitrary"),
        name="decay",
    )(zf, bf_row)


def _attn_kernel(qtab, jtab, q_ref, qa_ref, k_ref, ka_ref, v_ref, *rest, n_off, n_diag, n_cast):
    cast_in, rest = rest[:n_cast], rest[n_cast:]
    o_ref, rest = rest[0], rest[1:]
    cast_out, rest = rest[:n_cast], rest[n_cast:]
    for src, dst in zip(cast_in, cast_out):
        dst[...] = src[...].astype(BF16)
    _attn_body(qtab, jtab, q_ref, qa_ref, k_ref, ka_ref, v_ref, o_ref, *rest,
               n_off=n_off, n_diag=n_diag)


def _attn_body(qtab, jtab, q_ref, qa_ref, k_ref, ka_ref, v_ref, o_ref,
               qam_ref, m_ref, acc_ref, s_ref, mn_ref, a_ref, *, n_off, n_diag):
    seq = q_ref.shape[0]
    lane = lax.broadcasted_iota(jnp.int32, qa_ref.shape, 1)
    head_lane0 = pl.program_id(1) * DECAY_LANES
    mine = jnp.logical_and(lane >= head_lane0, lane < head_lane0 + DECAY_LANES)
    qam_ref[...] = jnp.where(mine, qa_ref[...], jnp.zeros(qa_ref.shape, BF16))
    m_ref[...] = jnp.full(m_ref.shape, -1e30, F32)
    acc_ref[...] = jnp.zeros(acc_ref.shape, F32)
    for slot in (ATTN_SLOTS - 2, ATTN_SLOTS - 1):
        s_ref[slot] = jnp.full((TQ, TK), -jnp.inf, F32)
        mn_ref[slot] = jnp.zeros((TQ, LANES), F32)
        a_ref[slot] = jnp.ones((TQ, LANES), F32)
    ones = jnp.ones((TK, HEAD_DIM), BF16)

    def q_rows(g):
        return pl.ds(pl.multiple_of(qtab[g] * TQ, TQ), TQ)

    def k_rows(g):
        return pl.ds(pl.multiple_of(jtab[g] * TK, TK), TK)

    def stage1(g, slot, masked):
        qr, kr = q_rows(g), k_rows(g)
        qp = jnp.concatenate([q_ref[qr, :], qam_ref[qr, :]], axis=1)
        kp = jnp.concatenate([k_ref[kr, :], ka_ref[kr, :]], axis=1)
        s = lax.dot_general(qp, kp, (((1,), (1,)), ((), ())), preferred_element_type=F32)
        if masked:
            r = lax.broadcasted_iota(jnp.int32, (TQ, TK), 0)
            c = lax.broadcasted_iota(jnp.int32, (TQ, TK), 1)
            s = jnp.where(c <= r, s, -jnp.inf)
        s_ref[slot] = s

    def stage2(g, slot):
        qr = q_rows(g)
        m_prev = m_ref[qr, :]
        m_new = jnp.maximum(m_prev, jnp.max(s_ref[slot], axis=1, keepdims=True))
        a_ref[slot] = jnp.exp2(m_prev - m_new)
        mn_ref[slot] = m_new
        m_ref[qr, :] = m_new

    def stage3(g, slot):
        qr, kr = q_rows(g), k_rows(g)
        m_new = mn_ref[slot]
        p = jnp.exp2(s_ref[slot] - jnp.concatenate([m_new] * (TK // LANES), axis=1))
        vp = jnp.concatenate([v_ref[kr, :], ones], axis=1)
        pv = jnp.dot(p.astype(BF16), vp, preferred_element_type=F32)
        alpha = a_ref[slot]
        acc_ref[qr, :] = acc_ref[qr, :] * jnp.concatenate([alpha, alpha], axis=1) + pv

    def body(i, c, base, masked):
        g0 = base + ATTN_SLOTS * i
        for u in range(ATTN_SLOTS):
            g = g0 + u
            stage3(jnp.maximum(g - 2, 0), (u + ATTN_SLOTS - 2) % ATTN_SLOTS)
            stage2(jnp.maximum(g - 1, 0), (u + ATTN_SLOTS - 1) % ATTN_SLOTS)
            stage1(g, u, masked)
        return c

    assert n_off % ATTN_SLOTS == 0 and n_diag % ATTN_SLOTS == 0 and ATTN_SLOTS >= 3
    total = n_off + n_diag
    lax.fori_loop(0, n_off // ATTN_SLOTS, functools.partial(body, base=0, masked=False), 0)
    lax.fori_loop(0, n_diag // ATTN_SLOTS, functools.partial(body, base=n_off, masked=True), 0)
    stage3(total - 2, ATTN_SLOTS - 2)
    stage2(total - 1, ATTN_SLOTS - 1)
    stage3(total - 1, ATTN_SLOTS - 1)

    def fin(r, c):
        sl = pl.ds(pl.multiple_of(r * TQ, TQ), TQ)
        acc = acc_ref[sl, :]
        o_ref[sl, :] = (acc[:, :HEAD_DIM] / acc[:, HEAD_DIM:]).astype(BF16)
        return c
    lax.fori_loop(0, seq // TQ, fin, 0)


def _attention(z, qa, ka, cast_weights, *, batch, seq, n_heads):
    assert TQ == TK
    nq = seq // TQ
    t = batch * seq
    steps = batch * n_heads
    cast_specs = [pl.BlockSpec((w.shape[0] // steps, w.shape[1]),
                               lambda b, h, qt, jt: (b * n_heads + h, 0)) for w in cast_weights]
    off = [(i, j) for i in range(nq) for j in range(i)]
    diag = [(i, i) for i in range(nq)]
    assert len(diag) >= 2
    qtab = jnp.asarray([p[0] for p in off + diag], jnp.int32)
    jtab = jnp.asarray([p[1] for p in off + diag], jnp.int32)
    grid_spec = pltpu.PrefetchScalarGridSpec(
        num_scalar_prefetch=2,
        grid=(batch, n_heads),
        in_specs=[
            pl.BlockSpec((seq, HEAD_DIM), lambda b, h, qt, jt: (b, h)),
            pl.BlockSpec((seq, LANES), lambda b, h, qt, jt: (b, 0)),
            pl.BlockSpec((seq, HEAD_DIM), lambda b, h, qt, jt: (b, n_heads + h)),
            pl.BlockSpec((seq, LANES), lambda b, h, qt, jt: (b, 0)),
            pl.BlockSpec((seq, HEAD_DIM), lambda b, h, qt, jt: (b, 2 * n_heads + h)),
        ] + cast_specs,
        out_specs=[pl.BlockSpec((seq, HEAD_DIM), lambda b, h, qt, jt: (b, h))] + cast_specs,
        scratch_shapes=[
            pltpu.VMEM((seq, LANES), BF16),
            pltpu.VMEM((seq, LANES), F32),
            pltpu.VMEM((seq, 2 * HEAD_DIM), F32),
            pltpu.VMEM((ATTN_SLOTS, TQ, TK), F32),
            pltpu.VMEM((ATTN_SLOTS, TQ, LANES), F32),
            pltpu.VMEM((ATTN_SLOTS, TQ, LANES), F32),
        ],
    )
    outs = pl.pallas_call(
        functools.partial(_attn_kernel, n_off=len(off), n_diag=len(diag),
                          n_cast=len(cast_weights)),
        grid_spec=grid_spec,
        out_shape=[jax.ShapeDtypeStruct((t, n_heads * HEAD_DIM), BF16)]
        + [jax.ShapeDtypeStruct(w.shape, BF16) for w in cast_weights],
        compiler_params=_params("parallel", "parallel"),
        name="attention",
    )(qtab, jtab, z, qa, z, ka, z, *cast_weights)
    return outs[0], outs[1:]


def _mixout_kernel(zu_ref, zv_ref, a_ref, x_ref, lng_ref, lnb_ref, ws_ref, bs_ref, og_ref,
                   ag_ref, w_ref, o_ref, w_sc, v_sc, gm_sc, gmn0, gmn1, an_sc):
    s = pl.program_id(0)
    tm = zu_ref.shape[0]
    n_heads = ws_ref.shape[0]
    d_attn = a_ref.shape[1]

    @pl.when(s == 0)
    def _():
        r = lax.broadcasted_iota(jnp.int32, (CHUNK, CHUNK), 0)
        c = lax.broadcasted_iota(jnp.int32, (CHUNK, CHUNK), 1)
        for h in range(n_heads):
            w_sc[h] = jnp.where(c <= r, ws_ref[h], 0.0).astype(BF16)
        gmn1[...] = jnp.zeros(gmn1.shape, BF16)

    def gate_values(cc):
        rows = slice(cc * CHUNK, (cc + 1) * CHUNK)
        gv = _gelu(zv_ref[rows, :].astype(F32))
        mu = jnp.mean(gv, axis=-1, keepdims=True)
        xc = gv - mu
        var = jnp.mean(xc * xc, axis=-1, keepdims=True)
        v_sc[rows, :] = (xc * lax.rsqrt(var + EPS) * lng_ref[...] + lnb_ref[...]).astype(BF16)

    def gate_mix(cc, gmn_w):
        rows = slice(cc * CHUNK, (cc + 1) * CHUNK)
        for h in range(n_heads):
            cols = slice(h * HEAD_DIM, (h + 1) * HEAD_DIM)
            mix = jnp.dot(w_sc[h], v_sc[rows, cols], preferred_element_type=F32) + bs_ref[h]
            gm_sc[rows, cols] = _gelu(zu_ref[rows, cols].astype(F32)) * mix
        gmn_w[rows, :] = _rms(gm_sc[rows, :], og_ref[...]).astype(BF16)

    def step(gmn_w, gmn_r):
        for r in range(tm // ROWS):
            sl = slice(r * ROWS, (r + 1) * ROWS)
            an_sc[sl, :] = _rms(a_ref[sl, :].astype(F32), ag_ref[...]).astype(BF16)
        n_pieces = 2 * (tm // CHUNK)
        ncol = o_ref.shape[1] // n_pieces
        for k in range(n_pieces):
            cols = slice(k * ncol, (k + 1) * ncol)
            y = (jnp.dot(an_sc[...], w_ref[:d_attn, cols], preferred_element_type=F32)
                 + jnp.dot(gmn_r[...], w_ref[d_attn:, cols], preferred_element_type=F32))
            o_ref[:, cols] = x_ref[:, cols] + y
            if k % 2 == 0:
                gate_values(k // 2)
            else:
                gate_mix(k // 2, gmn_w)

    @pl.when(s % 2 == 0)
    def _():
        step(gmn0, gmn1)

    @pl.when(s % 2 == 1)
    def _():
        step(gmn1, gmn0)


def _mixout(z, attn, x2d, ln_g, ln_b, w_s, b_s3, gm_g, attn_g, w_out, *, d_attn, d_gmlp):
    t, d = x2d.shape
    n_heads = w_s.shape[0]
    ucol = 3 * d_attn // d_gmlp
    n = t // TM_OUT
    gate_blk = lambda s: jnp.minimum(s, n - 1)
    proj_blk = lambda s: jnp.maximum(s - 1, 0)
    return pl.pallas_call(
        _mixout_kernel,
        grid=(n + 1,),
        in_specs=[
            pl.BlockSpec((TM_OUT, d_gmlp), lambda s: (gate_blk(s), ucol)),
            pl.BlockSpec((TM_OUT, d_gmlp), lambda s: (gate_blk(s), ucol + 1)),
            pl.BlockSpec((TM_OUT, d_attn), lambda s: (proj_blk(s), 0)),
            pl.BlockSpec((TM_OUT, d), lambda s: (proj_blk(s), 0)),
            pl.BlockSpec((1, d_gmlp), lambda i: (0, 0)),
            pl.BlockSpec((1, d_gmlp), lambda i: (0, 0)),
            pl.BlockSpec((n_heads, CHUNK, CHUNK), lambda i: (0, 0, 0)),
            pl.BlockSpec((n_heads, CHUNK, 1), lambda i: (0, 0, 0)),
            pl.BlockSpec((1, d_gmlp), lambda i: (0, 0)),
            pl.BlockSpec((1, d_attn), lambda i: (0, 0)),
            pl.BlockSpec((d_attn + d_gmlp, d), lambda i: (0, 0)),
        ],
        out_specs=pl.BlockSpec((TM_OUT, d), lambda s: (proj_blk(s), 0)),
        out_shape=jax.ShapeDtypeStruct((t, d), F32),
        scratch_shapes=[
            pltpu.VMEM((n_heads, CHUNK, CHUNK), BF16),
            pltpu.VMEM((TM_OUT, d_gmlp), BF16),
            pltpu.VMEM((TM_OUT, d_gmlp), F32),
            pltpu.VMEM((TM_OUT, d_gmlp), BF16),
            pltpu.VMEM((TM_OUT, d_gmlp), BF16),
            pltpu.VMEM((TM_OUT, d_attn), BF16),
        ],
        compiler_params=_params("arbitrary"),
        name="mixout",
    )(z, z, attn, x2d, ln_g, ln_b, w_s, b_s3, gm_g, attn_g, w_out)


def _ffn_kernel(x_ref, g_ref, w1_ref, w2_ref, gf_ref, o_ref, h_ref, *, final_norm):
    f = pl.program_id(1)
    last = pl.num_programs(1) - 1
    tm = x_ref.shape[0]

    def mlp():
        a = jnp.maximum(jnp.dot(h_ref[...], w1_ref[...], preferred_element_type=F32), 0.0)
        return jnp.dot((a * a).astype(BF16), w2_ref[...], preferred_element_type=F32)

    @pl.when(f == 0)
    def _():
        for r in range(tm // ROWS):
            sl = slice(r * ROWS, (r + 1) * ROWS)
            h_ref[sl, :] = _rms(x_ref[sl, :], g_ref[...]).astype(BF16)
        o_ref[...] = mlp()

    @pl.when(jnp.logical_and(f > 0, f < last))
    def _():
        o_ref[...] += mlp()

    @pl.when(f == last)
    def _():
        y = x_ref[...] + (o_ref[...] + mlp())
        if final_norm:
            for r in range(tm // ROWS):
                sl = slice(r * ROWS, (r + 1) * ROWS)
                o_ref[sl, :] = _rms(y[sl, :], gf_ref[...])
        else:
            o_ref[...] = y


def _ffn(x2d, g, w1, w2, g_final, *, final_norm):
    t, d = x2d.shape
    d_ff = w1.shape[1]
    assert d_ff // TF_FFN >= 2
    return pl.pallas_call(
        functools.partial(_ffn_kernel, final_norm=final_norm),
        grid=(t // TM_FFN, d_ff // TF_FFN),
        in_specs=[
            pl.BlockSpec((TM_FFN, d), lambda i, f: (i, 0)),
            pl.BlockSpec((1, d), lambda i, f: (0, 0)),
            pl.BlockSpec((d, TF_FFN), lambda i, f: (0, f)),
            pl.BlockSpec((TF_FFN, d), lambda i, f: (f, 0)),
            pl.BlockSpec((1, d), lambda i, f: (0, 0)),
        ],
        out_specs=pl.BlockSpec((TM_FFN, d), lambda i, f: (i, 0)),
        out_shape=jax.ShapeDtypeStruct((t, d), F32),
        scratch_shapes=[pltpu.VMEM((TM_FFN, d), BF16)],
        compiler_params=_params("parallel", "arbitrary"),
        name="ffn",
    )(x2d, g, w1, w2, g_final)


def kernel(x, norm_mix_g, w_in, b_f, gmlp_ln_g, gmlp_ln_b, w_s, b_s, attn_out_g, gmlp_out_g,
           w_out, norm_ffn_g, w_ff1, w_ff2, norm_final_g):
    batch, seq, d = x.shape
    depth = w_in.shape[0]
    n_heads = b_f.shape[1]
    d_attn = n_heads * HEAD_DIM
    d_gmlp = gmlp_ln_g.shape[1]
    o3 = 3 * d_attn
    o4 = o3 + n_heads
    assert d_attn == d_gmlp and w_in.shape[2] == o4 + 2 * d_gmlp

    x2d = x.reshape(batch * seq, d)
    for l in range(depth):
        w_main, w_gate = _wprep(jnp.swapaxes(w_in[l], 0, 1), o3=o3, n_gate=n_heads)
        w_f = jnp.swapaxes(_decay_lanes(jnp.swapaxes(w_gate, 0, 1), n_heads), 0, 1).astype(BF16)
        bf_row = _decay_lanes(b_f[l].astype(F32), n_heads).reshape(1, LANES)

        z, zf = _inproj(x2d, norm_mix_g[l].reshape(1, d), w_main, w_f, d_attn=d_attn)
        qa, ka = _decay(zf, bf_row, batch=batch, seq=seq, n_heads=n_heads)
        attn, (w_out16, w_ff1_16, w_ff2_16) = _attention(
            z, qa, ka, [w_out[l], w_ff1[l], w_ff2[l]], batch=batch, seq=seq, n_heads=n_heads)
        x2d = _mixout(z, attn, x2d, gmlp_ln_g[l].reshape(1, d_gmlp),
                      gmlp_ln_b[l].reshape(1, d_gmlp), w_s[l], b_s[l].reshape(n_heads, CHUNK, 1),
                      gmlp_out_g[l].reshape(1, d_gmlp), attn_out_g[l].reshape(1, d_attn),
                      w_out16, d_attn=d_attn, d_gmlp=d_gmlp)
        x2d = _ffn(x2d, norm_ffn_g[l].reshape(1, d), w_ff1_16, w_ff2_16,
                   norm_final_g.reshape(1, d), final_norm=(l == depth - 1))
    return x2d.reshape(batch, seq, d)
```

```python
import functools
import math

import numpy as np
import jax
import jax.numpy as jnp
from jax import lax
from jax.experimental import pallas as pl
from jax.experimental.pallas import tpu as pltpu

F32 = jnp.float32
BF16 = jnp.bfloat16

HEAD_DIM = 128
CHUNK = 128
EPS = 1e-6
LANES = 128
LOG2E = 1.4426950408889634
VMEM_LIMIT = 62 * 1024 * 1024

TM_WPREP = 256
TM_PROJ = 1024
TN_PROJ = 2560
TM_DECAY = 1024
TQ = 512
TK = 512
ATTN_SLOTS = 4
TM_OUT = 512
TM_FFN = 1024
TF_FFN = 1024
NORM_VREGS = 32
CUM_BLK = 256
DECAY_LANES = 8


def _row_groups(n_rows, n_cols, start=0):
    step = max(16, NORM_VREGS * 8 * LANES // n_cols)
    assert n_rows % step == 0
    return [slice(start + r, start + r + step) for r in range(0, n_rows, step)]


def _rms(x, g):
    ms = jnp.mean(x * x, axis=-1, keepdims=True)
    return x * lax.rsqrt(ms + EPS) * g


def _gelu(x):
    c = math.sqrt(2.0 / math.pi)
    inner = x * (c + (c * 0.044715) * (x * x))
    hx = 0.5 * x
    return hx + hx * jnp.tanh(inner)


def _split3(x):
    hi = x.astype(BF16)
    r1 = x - hi.astype(F32)
    mid = r1.astype(BF16)
    lo = (r1 - mid.astype(F32)).astype(BF16)
    return hi, mid, lo


def _params(*sem):
    return pltpu.CompilerParams(dimension_semantics=sem, vmem_limit_bytes=VMEM_LIMIT)


def _wprep_kernel(a_ref, b_ref, main_ref, gate_ref, *, gate_block, n_gate):
    j = pl.program_id(0)

    @pl.when(j < gate_block)
    def _():
        main_ref[...] = a_ref[...].astype(BF16)

    @pl.when(j >= gate_block)
    def _():
        main_ref[...] = jnp.concatenate([a_ref[n_gate:, :], b_ref[...]], axis=0).astype(BF16)

    @pl.when(j == gate_block)
    def _():
        gate_ref[...] = a_ref[:n_gate, :]


def _wprep(wt, *, o3, n_gate):
    n, d = wt.shape
    assert o3 % TM_WPREP == 0 and (n - n_gate) % TM_WPREP == 0 and TM_WPREP % n_gate == 0
    assert n_gate % 8 == 0
    per = TM_WPREP // n_gate
    return pl.pallas_call(
        functools.partial(_wprep_kernel, gate_block=o3 // TM_WPREP, n_gate=n_gate),
        grid=((n - n_gate) // TM_WPREP,),
        in_specs=[
            pl.BlockSpec((TM_WPREP, d), lambda j: (j, 0)),
            pl.BlockSpec((n_gate, d), lambda j: ((j + 1) * per, 0)),
        ],
        out_specs=[
            pl.BlockSpec((TM_WPREP, d), lambda j: (j, 0)),
            pl.BlockSpec((n_gate, d), lambda j: (0, 0)),
        ],
        out_shape=[
            jax.ShapeDtypeStruct((n - n_gate, d), BF16),
            jax.ShapeDtypeStruct((n_gate, d), F32),
        ],
        compiler_params=_params("arbitrary"),
        name="wprep",
    )(wt, wt)


def _inproj_kernel(x_ref, g_ref, w_ref, wf_ref, z_ref, zf_ref, h_ref, *, d_q, q_scale):
    j = pl.program_id(1)
    tm = x_ref.shape[0]

    nt = (((1,), (1,)), ((), ()))

    @pl.when(j == 0)
    def _():
        for sl in _row_groups(tm, x_ref.shape[1]):
            h_ref[sl, :] = _rms(x_ref[sl, :], g_ref[...]).astype(BF16)
        zf_ref[...] = lax.dot_general(h_ref[...], wf_ref[...], nt, preferred_element_type=F32)
        q = lax.dot_general(h_ref[...], w_ref[:d_q, :], nt, preferred_element_type=F32)
        z_ref[:, :d_q] = (q * q_scale).astype(BF16)
        z_ref[:, d_q:] = lax.dot_general(h_ref[...], w_ref[d_q:, :], nt,
                                         preferred_element_type=F32).astype(BF16)

    @pl.when(j > 0)
    def _():
        z_ref[...] = lax.dot_general(h_ref[...], w_ref[...], nt,
                                     preferred_element_type=F32).astype(BF16)


def _inproj(x2d, g, w_main, w_f, *, d_attn):
    t, d = x2d.shape
    n = w_main.shape[0]
    q_scale = LOG2E / math.sqrt(HEAD_DIM)
    assert d_attn < TN_PROJ and d_attn % LANES == 0
    return pl.pallas_call(
        functools.partial(_inproj_kernel, d_q=d_attn, q_scale=q_scale),
        grid=(t // TM_PROJ, n // TN_PROJ),
        in_specs=[
            pl.BlockSpec((TM_PROJ, d), lambda i, j: (i, 0)),
            pl.BlockSpec((1, d), lambda i, j: (0, 0)),
            pl.BlockSpec((TN_PROJ, d), lambda i, j: (j, 0)),
            pl.BlockSpec((LANES, d), lambda i, j: (0, 0)),
        ],
        out_specs=[
            pl.BlockSpec((TM_PROJ, TN_PROJ), lambda i, j: (i, j)),
            pl.BlockSpec((TM_PROJ, LANES), lambda i, j: (i, 0)),
        ],
        out_shape=[
            jax.ShapeDtypeStruct((t, n), BF16),
            jax.ShapeDtypeStruct((t, LANES), F32),
        ],
        scratch_shapes=[pltpu.VMEM((TM_PROJ, d), BF16)],
        compiler_params=_params("parallel", "arbitrary"),
        name="inproj",
    )(x2d, g, w_main, w_f)


def _decay_kernel(zf_ref, bf_ref, qa_ref, ka_ref, f_ref, carry_ref, *, n_heads):
    rows = zf_ref.shape[0]

    @pl.when(pl.program_id(1) == 0)
    def _():
        carry_ref[...] = jnp.zeros(carry_ref.shape, F32)

    row = lax.broadcasted_iota(jnp.int32, (CUM_BLK, CUM_BLK), 0)
    col = lax.broadcasted_iota(jnp.int32, (CUM_BLK, CUM_BLK), 1)
    tri = (col <= row).astype(BF16)
    carry = carry_ref[...]
    for r in range(rows // CUM_BLK):
        sl = slice(r * CUM_BLK, (r + 1) * CUM_BLK)
        x = zf_ref[sl, :] + bf_ref[...]
        logf = jnp.minimum(x, 0.0) - jnp.log1p(jnp.exp(-jnp.abs(x)))
        hi, mid, lo = _split3(logf)
        c = (jnp.dot(tri, hi, preferred_element_type=F32)
             + jnp.dot(tri, mid, preferred_element_type=F32)
             + jnp.dot(tri, lo, preferred_element_type=F32)) + carry
        f_ref[sl, :] = c
        carry = c[CUM_BLK - 1:CUM_BLK, :]
    carry_ref[...] = carry

    hi, mid, lo = _split3(f_ref[...] * LOG2E)
    lane = lax.broadcasted_iota(jnp.int32, (rows, LANES), 1)
    sub = lane & (DECAY_LANES - 1)
    used = lane < n_heads * DECAY_LANES
    piece_idx = jnp.where(sub >= 3, sub - 3, sub)
    piece = jnp.where(piece_idx == 0, hi, jnp.where(piece_idx == 1, mid, lo)).astype(F32)
    first = jnp.logical_and(used, sub < 3)
    second = jnp.logical_and(used, jnp.logical_and(sub >= 3, sub < 6))
    qa_ref[...] = jnp.where(first, piece, jnp.where(second, 1.0, 0.0)).astype(BF16)
    ka_ref[...] = jnp.where(first, 1.0, jnp.where(second, -piece, 0.0)).astype(BF16)


def _decay_lanes(v, n_heads):
    lead = v.shape[:-1]
    rep = jnp.broadcast_to(v[..., None], lead + (n_heads, 6))
    rep = jnp.pad(rep, [(0, 0)] * len(lead) + [(0, 0), (0, DECAY_LANES - 6)])
    rep = rep.reshape(lead + (n_heads * DECAY_LANES,))
    return jnp.pad(rep, [(0, 0)] * len(lead) + [(0, LANES - n_heads * DECAY_LANES)])


def _decay(zf, bf_row, *, batch, seq, n_heads):
    assert n_heads * DECAY_LANES <= LANES
    ns = seq // TM_DECAY
    return pl.pallas_call(
        functools.partial(_decay_kernel, n_heads=n_heads),
        grid=(batch, ns),
        in_specs=[
            pl.BlockSpec((TM_DECAY, LANES), lambda b, s: (b * ns + s, 0)),
            pl.BlockSpec((1, LANES), lambda b, s: (0, 0)),
        ],
        out_specs=[
            pl.BlockSpec((TM_DECAY, LANES), lambda b, s: (b * ns + s, 0)),
            pl.BlockSpec((TM_DECAY, LANES), lambda b, s: (b * ns + s, 0)),
        ],
        out_shape=[
            jax.ShapeDtypeStruct((batch * seq, LANES), BF16),
            jax.ShapeDtypeStruct((batch * seq, LANES), BF16),
        ],
        scratch_shapes=[pltpu.VMEM((TM_DECAY, LANES), F32), pltpu.VMEM((1, LANES), F32)],
        compiler_params=_params("parallel", "arbitrary"),
        name="decay",
    )(zf, bf_row)


def _attn_kernel(q_ref, qa_ref, k_ref, ka_ref, v_ref, *rest, pairs, n_cast):
    cast_in, rest = rest[:n_cast], rest[n_cast:]
    o_ref, rest = rest[0], rest[1:]
    cast_out, rest = rest[:n_cast], rest[n_cast:]
    for src, dst in zip(cast_in, cast_out):
        dst[...] = src[...].astype(BF16)
    _attn_body(q_ref, qa_ref, k_ref, ka_ref, v_ref, o_ref, *rest, pairs=pairs)


def _attn_body(q_ref, qa_ref, k_ref, ka_ref, v_ref, o_ref,
               qam_ref, m_ref, acc_ref, s_ref, mn_ref, a_ref, *, pairs):
    lane = lax.broadcasted_iota(jnp.int32, qa_ref.shape, 1)
    head_lane0 = pl.program_id(1) * DECAY_LANES
    mine = jnp.logical_and(lane >= head_lane0, lane < head_lane0 + DECAY_LANES)
    qam_ref[...] = jnp.where(mine, qa_ref[...], jnp.zeros(qa_ref.shape, BF16))
    ones = jnp.ones((TK, HEAD_DIM), BF16)

    def rows(blk, size):
        return slice(blk * size, (blk + 1) * size)

    def stage1(g):
        qi, kj = pairs[g]
        qr, kr = rows(qi, TQ), rows(kj, TK)
        qp = jnp.concatenate([q_ref[qr, :], qam_ref[qr, :]], axis=1)
        kp = jnp.concatenate([k_ref[kr, :], ka_ref[kr, :]], axis=1)
        s = lax.dot_general(qp, kp, (((1,), (1,)), ((), ())), preferred_element_type=F32)
        if qi == kj:
            r = lax.broadcasted_iota(jnp.int32, (TQ, TK), 0)
            c = lax.broadcasted_iota(jnp.int32, (TQ, TK), 1)
            s = jnp.where(c <= r, s, -jnp.inf)
        s_ref[g % ATTN_SLOTS] = s

    def stage2(g):
        qi, _ = pairs[g]
        qr, slot = rows(qi, TQ), g % ATTN_SLOTS
        bm = jnp.max(s_ref[slot], axis=1, keepdims=True)
        if first[g]:
            m_new = jnp.broadcast_to(bm, (TQ, LANES))
        else:
            m_prev = m_ref[qr, :]
            m_new = jnp.maximum(m_prev, bm)
            a_ref[slot] = jnp.exp2(m_prev - m_new)
        mn_ref[slot] = m_new
        m_ref[qr, :] = m_new

    def stage3(g):
        qi, kj = pairs[g]
        qr, kr, slot = rows(qi, TQ), rows(kj, TK), g % ATTN_SLOTS
        m_new = mn_ref[slot]
        p = jnp.exp2(s_ref[slot] - jnp.concatenate([m_new] * (TK // LANES), axis=1))
        vp = jnp.concatenate([v_ref[kr, :], ones], axis=1)
        pv = jnp.dot(p.astype(BF16), vp, preferred_element_type=F32)
        if first[g]:
            acc = pv
        else:
            alpha = a_ref[slot]
            acc = acc_ref[qr, :] * jnp.concatenate([alpha, alpha], axis=1) + pv
        if qi == kj:
            o_ref[qr, :] = (acc[:, :HEAD_DIM] / acc[:, HEAD_DIM:]).astype(BF16)
        else:
            acc_ref[qr, :] = acc

    seen = set()
    first = []
    for qi, _ in pairs:
        first.append(qi not in seen)
        seen.add(qi)
    total = len(pairs)
    for g in range(total + 2):
        if g >= 2:
            stage3(g - 2)
        if 1 <= g <= total:
            stage2(g - 1)
        if g < total:
            stage1(g)


def _attention(z, qa, ka, cast_weights, *, batch, seq, n_heads):
    assert TQ == TK and ATTN_SLOTS >= 3
    nq = seq // TQ
    t = batch * seq
    steps = batch * n_heads
    cast_specs = [pl.BlockSpec((w.shape[0] // steps, w.shape[1]),
                               lambda b, h: (b * n_heads + h, 0)) for w in cast_weights]
    pairs = tuple([(i, j) for i in range(nq) for j in range(i)] + [(i, i) for i in range(nq)])
    outs = pl.pallas_call(
        functools.partial(_attn_kernel, pairs=pairs, n_cast=len(cast_weights)),
        grid=(batch, n_heads),
        in_specs=[
            pl.BlockSpec((seq, HEAD_DIM), lambda b, h: (b, h)),
            pl.BlockSpec((seq, LANES), lambda b, h: (b, 0)),
            pl.BlockSpec((seq, HEAD_DIM), lambda b, h: (b, n_heads + h)),
            pl.BlockSpec((seq, LANES), lambda b, h: (b, 0)),
            pl.BlockSpec((seq, HEAD_DIM), lambda b, h: (b, 2 * n_heads + h)),
        ] + cast_specs,
        out_specs=[pl.BlockSpec((seq, HEAD_DIM), lambda b, h: (b, h))] + cast_specs,
        out_shape=[jax.ShapeDtypeStruct((t, n_heads * HEAD_DIM), BF16)]
        + [jax.ShapeDtypeStruct(w.shape, BF16) for w in cast_weights],
        scratch_shapes=[
            pltpu.VMEM((seq, LANES), BF16),
            pltpu.VMEM((seq, LANES), F32),
            pltpu.VMEM((seq, 2 * HEAD_DIM), F32),
            pltpu.VMEM((ATTN_SLOTS, TQ, TK), F32),
            pltpu.VMEM((ATTN_SLOTS, TQ, LANES), F32),
            pltpu.VMEM((ATTN_SLOTS, TQ, LANES), F32),
        ],
        compiler_params=_params("parallel", "parallel"),
        name="attention",
    )(z, qa, z, ka, z, *cast_weights)
    return outs[0], outs[1:]


def _mixout_kernel(zu_ref, zv_ref, a_ref, x_ref, lng_ref, lnb_ref, ws_ref, bs_ref, og_ref,
                   ag_ref, w_ref, o_ref, w_sc, v_sc, gm_sc, gmn0, gmn1, an_sc):
    s = pl.program_id(0)
    tm = zu_ref.shape[0]
    n_heads = ws_ref.shape[0]
    d_attn = a_ref.shape[1]

    @pl.when(s == 0)
    def _():
        r = lax.broadcasted_iota(jnp.int32, (CHUNK, CHUNK), 0)
        c = lax.broadcasted_iota(jnp.int32, (CHUNK, CHUNK), 1)
        for h in range(n_heads):
            w_sc[h] = jnp.where(c <= r, ws_ref[h], 0.0).astype(BF16)
        gmn1[...] = jnp.zeros(gmn1.shape, BF16)

    def gate_values(cc):
        for rows in _row_groups(CHUNK, zv_ref.shape[1], cc * CHUNK):
            gv = _gelu(zv_ref[rows, :].astype(F32))
            mu = jnp.mean(gv, axis=-1, keepdims=True)
            xc = gv - mu
            var = jnp.mean(xc * xc, axis=-1, keepdims=True)
            v_sc[rows, :] = (xc * lax.rsqrt(var + EPS) * lng_ref[...]
                             + lnb_ref[...]).astype(BF16)

    def gate_mix(cc, gmn_w):
        rows = slice(cc * CHUNK, (cc + 1) * CHUNK)
        for h in range(n_heads):
            cols = slice(h * HEAD_DIM, (h + 1) * HEAD_DIM)
            mix = jnp.dot(w_sc[h], v_sc[rows, cols], preferred_element_type=F32) + bs_ref[h]
            gm_sc[rows, cols] = _gelu(zu_ref[rows, cols].astype(F32)) * mix
        for sl in _row_groups(CHUNK, gm_sc.shape[1], cc * CHUNK):
            gmn_w[sl, :] = _rms(gm_sc[sl, :], og_ref[...]).astype(BF16)

    def step(gmn_w, gmn_r):
        for sl in _row_groups(tm, d_attn):
            an_sc[sl, :] = _rms(a_ref[sl, :].astype(F32), ag_ref[...]).astype(BF16)
        n_pieces = 2 * (tm // CHUNK)
        ncol = o_ref.shape[1] // n_pieces
        for k in range(n_pieces):
            cols = slice(k * ncol, (k + 1) * ncol)
            y = (jnp.dot(an_sc[...], w_ref[:d_attn, cols], preferred_element_type=F32)
                 + jnp.dot(gmn_r[...], w_ref[d_attn:, cols], preferred_element_type=F32))
            o_ref[:, cols] = x_ref[:, cols] + y
            if k % 2 == 0:
                gate_values(k // 2)
            else:
                gate_mix(k // 2, gmn_w)

    @pl.when(s % 2 == 0)
    def _():
        step(gmn0, gmn1)

    @pl.when(s % 2 == 1)
    def _():
        step(gmn1, gmn0)


def _mixout(z, attn, x2d, ln_g, ln_b, w_s, b_s3, gm_g, attn_g, w_out, *, d_attn, d_gmlp):
    t, d = x2d.shape
    n_heads = w_s.shape[0]
    ucol = 3 * d_attn // d_gmlp
    n = t // TM_OUT
    gate_blk = lambda s: jnp.minimum(s, n - 1)
    proj_blk = lambda s: jnp.maximum(s - 1, 0)
    return pl.pallas_call(
        _mixout_kernel,
        grid=(n + 1,),
        in_specs=[
            pl.BlockSpec((TM_OUT, d_gmlp), lambda s: (gate_blk(s), ucol)),
            pl.BlockSpec((TM_OUT, d_gmlp), lambda s: (gate_blk(s), ucol + 1)),
            pl.BlockSpec((TM_OUT, d_attn), lambda s: (proj_blk(s), 0)),
            pl.BlockSpec((TM_OUT, d), lambda s: (proj_blk(s), 0)),
            pl.BlockSpec((1, d_gmlp), lambda i: (0, 0)),
            pl.BlockSpec((1, d_gmlp), lambda i: (0, 0)),
            pl.BlockSpec((n_heads, CHUNK, CHUNK), lambda i: (0, 0, 0)),
            pl.BlockSpec((n_heads, CHUNK, 1), lambda i: (0, 0, 0)),
            pl.BlockSpec((1, d_gmlp), lambda i: (0, 0)),
            pl.BlockSpec((1, d_attn), lambda i: (0, 0)),
            pl.BlockSpec((d_attn + d_gmlp, d), lambda i: (0, 0)),
        ],
        out_specs=pl.BlockSpec((TM_OUT, d), lambda s: (proj_blk(s), 0)),
        out_shape=jax.ShapeDtypeStruct((t, d), F32),
        scratch_shapes=[
            pltpu.VMEM((n_heads, CHUNK, CHUNK), BF16),
            pltpu.VMEM((TM_OUT, d_gmlp), BF16),
            pltpu.VMEM((TM_OUT, d_gmlp), F32),
            pltpu.VMEM((TM_OUT, d_gmlp), BF16),
            pltpu.VMEM((TM_OUT, d_gmlp), BF16),
            pltpu.VMEM((TM_OUT, d_attn), BF16),
        ],
        compiler_params=_params("arbitrary"),
        name="mixout",
    )(z, z, attn, x2d, ln_g, ln_b, w_s, b_s3, gm_g, attn_g, w_out)


def _ffn_kernel(x_ref, g_ref, w1_ref, w2_ref, gf_ref, o_ref, h_ref, *, final_norm):
    f = pl.program_id(1)
    last = pl.num_programs(1) - 1
    tm = x_ref.shape[0]

    def hidden():
        a = jnp.maximum(jnp.dot(h_ref[...], w1_ref[...], preferred_element_type=F32), 0.0)
        return (a * a).astype(BF16)

    def mlp():
        return jnp.dot(hidden(), w2_ref[...], preferred_element_type=F32)

    @pl.when(f == 0)
    def _():
        for sl in _row_groups(tm, x_ref.shape[1]):
            h_ref[sl, :] = _rms(x_ref[sl, :], g_ref[...]).astype(BF16)
        o_ref[...] = mlp()

    @pl.when(jnp.logical_and(f > 0, f < last))
    def _():
        o_ref[...] += mlp()

    @pl.when(f == last)
    def _():
        a = hidden()
        half = tm // 2
        for r0 in (0, half):
            rows = slice(r0, r0 + half)
            y = x_ref[rows, :] + (o_ref[rows, :] + jnp.dot(
                a[rows, :], w2_ref[...], preferred_element_type=F32))
            if final_norm:
                for sl in _row_groups(half, y.shape[1]):
                    o_ref[r0 + sl.start:r0 + sl.stop, :] = _rms(y[sl, :], gf_ref[...])
            else:
                o_ref[rows, :] = y


def _ffn(x2d, g, w1, w2, g_final, *, final_norm):
    t, d = x2d.shape
    d_ff = w1.shape[1]
    assert d_ff // TF_FFN >= 2
    return pl.pallas_call(
        functools.partial(_ffn_kernel, final_norm=final_norm),
        grid=(t // TM_FFN, d_ff // TF_FFN),
        in_specs=[
            pl.BlockSpec((TM_FFN, d), lambda i, f: (i, 0)),
            pl.BlockSpec((1, d), lambda i, f: (0, 0)),
            pl.BlockSpec((d, TF_FFN), lambda i, f: (0, f)),
            pl.BlockSpec((TF_FFN, d), lambda i, f: (f, 0)),
            pl.BlockSpec((1, d), lambda i, f: (0, 0)),
        ],
        out_specs=pl.BlockSpec((TM_FFN, d), lambda i, f: (i, 0)),
        out_shape=jax.ShapeDtypeStruct((t, d), F32),
        scratch_shapes=[pltpu.VMEM((TM_FFN, d), BF16)],
        compiler_params=_params("parallel", "arbitrary"),
        name="ffn",
    )(x2d, g, w1, w2, g_final)


def kernel(x, norm_mix_g, w_in, b_f, gmlp_ln_g, gmlp_ln_b, w_s, b_s, attn_out_g, gmlp_out_g,
           w_out, norm_ffn_g, w_ff1, w_ff2, norm_final_g):
    batch, seq, d = x.shape
    depth = w_in.shape[0]
    n_heads = b_f.shape[1]
    d_attn = n_heads * HEAD_DIM
    d_gmlp = gmlp_ln_g.shape[1]
    o3 = 3 * d_attn
    o4 = o3 + n_heads
    assert d_attn == d_gmlp and w_in.shape[2] == o4 + 2 * d_gmlp

    x2d = x.reshape(batch * seq, d)
    for l in range(depth):
        w_main, w_gate = _wprep(jnp.swapaxes(w_in[l], 0, 1), o3=o3, n_gate=n_heads)
        w_f = jnp.swapaxes(_decay_lanes(jnp.swapaxes(w_gate, 0, 1), n_heads), 0, 1).astype(BF16)
        bf_row = _decay_lanes(b_f[l].astype(F32), n_heads).reshape(1, LANES)

        z, zf = _inproj(x2d, norm_mix_g[l].reshape(1, d), w_main, w_f, d_attn=d_attn)
        qa, ka = _decay(zf, bf_row, batch=batch, seq=seq, n_heads=n_heads)
        attn, (w_out16, w_ff1_16, w_ff2_16) = _attention(
            z, qa, ka, [w_out[l], w_ff1[l], w_ff2[l]], batch=batch, seq=seq, n_heads=n_heads)
        x2d = _mixout(z, attn, x2d, gmlp_ln_g[l].reshape(1, d_gmlp),
                      gmlp_ln_b[l].reshape(1, d_gmlp), w_s[l], b_s[l].reshape(n_heads, CHUNK, 1),
                      gmlp_out_g[l].reshape(1, d_gmlp), attn_out_g[l].reshape(1, d_attn),
                      w_out16, d_attn=d_attn, d_gmlp=d_gmlp)
        x2d = _ffn(x2d, norm_ffn_g[l].reshape(1, d), w_ff1_16, w_ff2_16,
                   norm_final_g.reshape(1, d), final_norm=(l == depth - 1))
    return x2d.reshape(batch, seq, d)
```

```python
import functools
import math

import jax
import jax.numpy as jnp
from jax import lax
from jax.experimental import pallas as pl
from jax.experimental.pallas import tpu as pltpu

F32 = jnp.float32
BF16 = jnp.bfloat16

HEAD_DIM = 128
CHUNK = 128
EPS = 1e-6
LANES = 128
LOG2E = 1.4426950408889634
VMEM_LIMIT = 62 * 1024 * 1024

TM_WPREP = 512
TM_PROJ = 1024
TN_PROJ = 2560
TM_DECAY = 2048
TQ = 512
TK = 512
ATTN_SLOTS = 4
TM_OUT = 512
TM_FFN = 1024
TF_FFN = 1024
FFN_LAST_PARTS = 4
NORM_VREGS = 32
CUM_BLK = 256
DECAY_LANES = 8


def _row_groups(n_rows, n_cols, start=0):
    step = max(16, NORM_VREGS * 8 * LANES // n_cols)
    assert n_rows % step == 0
    return [slice(start + r, start + r + step) for r in range(0, n_rows, step)]


def _rms(x, g):
    ms = jnp.mean(x * x, axis=-1, keepdims=True)
    return x * lax.rsqrt(ms + EPS) * g


def _gelu(x):
    c = math.sqrt(2.0 / math.pi)
    inner = x * (c + (c * 0.044715) * (x * x))
    hx = 0.5 * x
    return hx + hx * jnp.tanh(inner)


def _split3(x):
    hi = x.astype(BF16)
    r1 = x - hi.astype(F32)
    mid = r1.astype(BF16)
    lo = (r1 - mid.astype(F32)).astype(BF16)
    return hi, mid, lo


def _params(*sem):
    return pltpu.CompilerParams(dimension_semantics=sem, vmem_limit_bytes=VMEM_LIMIT)


def _wprep_kernel(a_ref, b_ref, main_ref, gate_ref, *, gate_block, n_gate):
    j = pl.program_id(0)

    @pl.when(j < gate_block)
    def _():
        main_ref[...] = a_ref[...].astype(BF16)

    @pl.when(j >= gate_block)
    def _():
        main_ref[...] = jnp.concatenate([a_ref[n_gate:, :], b_ref[...]], axis=0).astype(BF16)

    @pl.when(j == gate_block)
    def _():
        gate_ref[...] = a_ref[:n_gate, :]


def _wprep(wt, *, o3, n_gate):
    n, d = wt.shape
    assert o3 % TM_WPREP == 0 and (n - n_gate) % TM_WPREP == 0 and TM_WPREP % n_gate == 0
    assert n_gate % 8 == 0
    per = TM_WPREP // n_gate
    return pl.pallas_call(
        functools.partial(_wprep_kernel, gate_block=o3 // TM_WPREP, n_gate=n_gate),
        grid=((n - n_gate) // TM_WPREP,),
        in_specs=[
            pl.BlockSpec((TM_WPREP, d), lambda j: (j, 0)),
            pl.BlockSpec((n_gate, d), lambda j: ((j + 1) * per, 0)),
        ],
        out_specs=[
            pl.BlockSpec((TM_WPREP, d), lambda j: (j, 0)),
            pl.BlockSpec((n_gate, d), lambda j: (0, 0)),
        ],
        out_shape=[
            jax.ShapeDtypeStruct((n - n_gate, d), BF16),
            jax.ShapeDtypeStruct((n_gate, d), F32),
        ],
        compiler_params=_params("arbitrary"),
        name="wprep",
    )(wt, wt)


def _inproj_kernel(x_ref, g_ref, w_ref, wf_ref, z_ref, zf_ref, h_ref, *, d_q, q_scale):
    j = pl.program_id(1)
    tm = x_ref.shape[0]

    nt = (((1,), (1,)), ((), ()))

    @pl.when(j == 0)
    def _():
        for sl in _row_groups(tm, x_ref.shape[1]):
            h_ref[sl, :] = _rms(x_ref[sl, :], g_ref[...]).astype(BF16)
        zf_ref[...] = lax.dot_general(h_ref[...], wf_ref[...], nt, preferred_element_type=F32)
        q = lax.dot_general(h_ref[...], w_ref[:d_q, :], nt, preferred_element_type=F32)
        z_ref[:, :d_q] = (q * q_scale).astype(BF16)
        z_ref[:, d_q:] = lax.dot_general(h_ref[...], w_ref[d_q:, :], nt,
                                         preferred_element_type=F32).astype(BF16)

    @pl.when(j > 0)
    def _():
        z_ref[...] = lax.dot_general(h_ref[...], w_ref[...], nt,
                                     preferred_element_type=F32).astype(BF16)


def _inproj(x2d, g, w_main, w_f, *, d_attn):
    t, d = x2d.shape
    n = w_main.shape[0]
    q_scale = LOG2E / math.sqrt(HEAD_DIM)
    assert d_attn < TN_PROJ and d_attn % LANES == 0
    return pl.pallas_call(
        functools.partial(_inproj_kernel, d_q=d_attn, q_scale=q_scale),
        grid=(t // TM_PROJ, n // TN_PROJ),
        in_specs=[
            pl.BlockSpec((TM_PROJ, d), lambda i, j: (i, 0)),
            pl.BlockSpec((1, d), lambda i, j: (0, 0)),
            pl.BlockSpec((TN_PROJ, d), lambda i, j: (j, 0)),
            pl.BlockSpec((LANES, d), lambda i, j: (0, 0)),
        ],
        out_specs=[
            pl.BlockSpec((TM_PROJ, TN_PROJ), lambda i, j: (i, j)),
            pl.BlockSpec((TM_PROJ, LANES), lambda i, j: (i, 0)),
        ],
        out_shape=[
            jax.ShapeDtypeStruct((t, n), BF16),
            jax.ShapeDtypeStruct((t, LANES), F32),
        ],
        scratch_shapes=[pltpu.VMEM((TM_PROJ, d), BF16)],
        compiler_params=_params("parallel", "arbitrary"),
        name="inproj",
    )(x2d, g, w_main, w_f)


def _decay_kernel(zf_ref, bf_ref, qa_ref, ka_ref, f_ref, carry_ref, *, n_heads):
    rows = zf_ref.shape[0]

    @pl.when(pl.program_id(1) == 0)
    def _():
        carry_ref[...] = jnp.zeros(carry_ref.shape, F32)

    row = lax.broadcasted_iota(jnp.int32, (CUM_BLK, CUM_BLK), 0)
    col = lax.broadcasted_iota(jnp.int32, (CUM_BLK, CUM_BLK), 1)
    tri = (col <= row).astype(BF16)
    carry = carry_ref[...]
    for r in range(rows // CUM_BLK):
        sl = slice(r * CUM_BLK, (r + 1) * CUM_BLK)
        x = zf_ref[sl, :] + bf_ref[...]
        logf = jnp.minimum(x, 0.0) - jnp.log1p(jnp.exp(-jnp.abs(x)))
        hi, mid, lo = _split3(logf)
        c = (jnp.dot(tri, hi, preferred_element_type=F32)
             + jnp.dot(tri, mid, preferred_element_type=F32)
             + jnp.dot(tri, lo, preferred_element_type=F32)) + carry
        f_ref[sl, :] = c
        carry = c[CUM_BLK - 1:CUM_BLK, :]
    carry_ref[...] = carry

    hi, mid, lo = _split3(f_ref[...] * LOG2E)
    lane = lax.broadcasted_iota(jnp.int32, (rows, LANES), 1)
    sub = lane & (DECAY_LANES - 1)
    used = lane < n_heads * DECAY_LANES
    piece_idx = jnp.where(sub >= 3, sub - 3, sub)
    piece = jnp.where(piece_idx == 0, hi, jnp.where(piece_idx == 1, mid, lo)).astype(F32)
    first = jnp.logical_and(used, sub < 3)
    second = jnp.logical_and(used, jnp.logical_and(sub >= 3, sub < 6))
    qa_ref[...] = jnp.where(first, piece, jnp.where(second, 1.0, 0.0)).astype(BF16)
    ka_ref[...] = jnp.where(first, 1.0, jnp.where(second, -piece, 0.0)).astype(BF16)


def _decay_lanes(v, n_heads):
    lead = v.shape[:-1]
    rep = jnp.broadcast_to(v[..., None], lead + (n_heads, 6))
    rep = jnp.pad(rep, [(0, 0)] * len(lead) + [(0, 0), (0, DECAY_LANES - 6)])
    rep = rep.reshape(lead + (n_heads * DECAY_LANES,))
    return jnp.pad(rep, [(0, 0)] * len(lead) + [(0, LANES - n_heads * DECAY_LANES)])


def _decay(zf, bf_row, *, batch, seq, n_heads):
    assert n_heads * DECAY_LANES <= LANES
    ns = seq // TM_DECAY
    return pl.pallas_call(
        functools.partial(_decay_kernel, n_heads=n_heads),
        grid=(batch, ns),
        in_specs=[
            pl.BlockSpec((TM_DECAY, LANES), lambda b, s: (b * ns + s, 0)),
            pl.BlockSpec((1, LANES), lambda b, s: (0, 0)),
        ],
        out_specs=[
            pl.BlockSpec((TM_DECAY, LANES), lambda b, s: (b * ns + s, 0)),
            pl.BlockSpec((TM_DECAY, LANES), lambda b, s: (b * ns + s, 0)),
        ],
        out_shape=[
            jax.ShapeDtypeStruct((batch * seq, LANES), BF16),
            jax.ShapeDtypeStruct((batch * seq, LANES), BF16),
        ],
        scratch_shapes=[pltpu.VMEM((TM_DECAY, LANES), F32), pltpu.VMEM((1, LANES), F32)],
        compiler_params=_params("parallel", "arbitrary"),
        name="decay",
    )(zf, bf_row)


def _attn_kernel(q_ref, qa_ref, k_ref, ka_ref, v_ref, *rest, pairs, n_cast):
    cast_in, rest = rest[:n_cast], rest[n_cast:]
    o_ref, rest = rest[0], rest[1:]
    cast_out, rest = rest[:n_cast], rest[n_cast:]
    for src, dst in zip(cast_in, cast_out):
        dst[...] = src[...].astype(BF16)
    _attn_body(q_ref, qa_ref, k_ref, ka_ref, v_ref, o_ref, *rest, pairs=pairs)


def _attn_body(q_ref, qa_ref, k_ref, ka_ref, v_ref, o_ref,
               qam_ref, m_ref, acc_ref, s_ref, mn_ref, a_ref, *, pairs):
    lane = lax.broadcasted_iota(jnp.int32, qa_ref.shape, 1)
    head_lane0 = pl.program_id(1) * DECAY_LANES
    mine = jnp.logical_and(lane >= head_lane0, lane < head_lane0 + DECAY_LANES)
    qam_ref[...] = jnp.where(mine, qa_ref[...], jnp.zeros(qa_ref.shape, BF16))
    ones = jnp.ones((TK, HEAD_DIM), BF16)

    def rows(blk, size):
        return slice(blk * size, (blk + 1) * size)

    def stage1(g):
        qi, kj = pairs[g]
        qr, kr = rows(qi, TQ), rows(kj, TK)
        qp = jnp.concatenate([q_ref[qr, :], qam_ref[qr, :]], axis=1)
        kp = jnp.concatenate([k_ref[kr, :], ka_ref[kr, :]], axis=1)
        s = lax.dot_general(qp, kp, (((1,), (1,)), ((), ())), preferred_element_type=F32)
        if qi == kj:
            r = lax.broadcasted_iota(jnp.int32, (TQ, TK), 0)
            c = lax.broadcasted_iota(jnp.int32, (TQ, TK), 1)
            s = jnp.where(c <= r, s, -jnp.inf)
        s_ref[g % ATTN_SLOTS] = s

    def stage2(g):
        qi, _ = pairs[g]
        qr, slot = rows(qi, TQ), g % ATTN_SLOTS
        bm = jnp.max(s_ref[slot], axis=1, keepdims=True)
        if first[g]:
            m_new = jnp.broadcast_to(bm, (TQ, LANES))
        else:
            m_prev = m_ref[qr, :]
            m_new = jnp.maximum(m_prev, bm)
            a_ref[slot] = jnp.exp2(m_prev - m_new)
        mn_ref[slot] = m_new
        m_ref[qr, :] = m_new

    def stage3(g):
        qi, kj = pairs[g]
        qr, kr, slot = rows(qi, TQ), rows(kj, TK), g % ATTN_SLOTS
        m_new = mn_ref[slot]
        p = jnp.exp2(s_ref[slot] - jnp.concatenate([m_new] * (TK // LANES), axis=1))
        vp = jnp.concatenate([v_ref[kr, :], ones], axis=1)
        pv = jnp.dot(p.astype(BF16), vp, preferred_element_type=F32)
        if first[g]:
            acc = pv
        else:
            alpha = a_ref[slot]
            acc = acc_ref[qr, :] * jnp.concatenate([alpha, alpha], axis=1) + pv
        if qi == kj:
            o_ref[qr, :] = (acc[:, :HEAD_DIM] / acc[:, HEAD_DIM:]).astype(BF16)
        else:
            acc_ref[qr, :] = acc

    seen = set()
    first = []
    for qi, _ in pairs:
        first.append(qi not in seen)
        seen.add(qi)
    total = len(pairs)
    for g in range(total + 2):
        if g >= 2:
            stage3(g - 2)
        if 1 <= g <= total:
            stage2(g - 1)
        if g < total:
            stage1(g)


def _attention(z, qa, ka, cast_weights, *, batch, seq, n_heads):
    assert TQ == TK and ATTN_SLOTS >= 3
    nq = seq // TQ
    t = batch * seq
    steps = batch * n_heads
    cast_specs = [pl.BlockSpec((w.shape[0] // steps, w.shape[1]),
                               lambda b, h: (b * n_heads + h, 0)) for w in cast_weights]
    pairs = tuple([(i, j) for i in range(nq) for j in range(i)] + [(i, i) for i in range(nq)])
    outs = pl.pallas_call(
        functools.partial(_attn_kernel, pairs=pairs, n_cast=len(cast_weights)),
        grid=(batch, n_heads),
        in_specs=[
            pl.BlockSpec((seq, HEAD_DIM), lambda b, h: (b, h)),
            pl.BlockSpec((seq, LANES), lambda b, h: (b, 0)),
            pl.BlockSpec((seq, HEAD_DIM), lambda b, h: (b, n_heads + h)),
            pl.BlockSpec((seq, LANES), lambda b, h: (b, 0)),
            pl.BlockSpec((seq, HEAD_DIM), lambda b, h: (b, 2 * n_heads + h)),
        ] + cast_specs,
        out_specs=[pl.BlockSpec((seq, HEAD_DIM), lambda b, h: (b, h))] + cast_specs,
        out_shape=[jax.ShapeDtypeStruct((t, n_heads * HEAD_DIM), BF16)]
        + [jax.ShapeDtypeStruct(w.shape, BF16) for w in cast_weights],
        scratch_shapes=[
            pltpu.VMEM((seq, LANES), BF16),
            pltpu.VMEM((seq, LANES), F32),
            pltpu.VMEM((seq, 2 * HEAD_DIM), F32),
            pltpu.VMEM((ATTN_SLOTS, TQ, TK), F32),
            pltpu.VMEM((ATTN_SLOTS, TQ, LANES), F32),
            pltpu.VMEM((ATTN_SLOTS, TQ, LANES), F32),
        ],
        compiler_params=_params("parallel", "parallel"),
        name="attention",
    )(z, qa, z, ka, z, *cast_weights)
    return outs[0], outs[1:]


def _mixout_kernel(zu_ref, zv_ref, a_ref, x_ref, lng_ref, lnb_ref, ws_ref, bs_ref, og_ref,
                   ag_ref, w_ref, o_ref, w_sc, v_sc, gm_sc, gmn0, gmn1, an_sc):
    s = pl.program_id(0)
    tm = zu_ref.shape[0]
    n_heads = ws_ref.shape[0]
    d_attn = a_ref.shape[1]

    @pl.when(s == 0)
    def _():
        r = lax.broadcasted_iota(jnp.int32, (CHUNK, CHUNK), 0)
        c = lax.broadcasted_iota(jnp.int32, (CHUNK, CHUNK), 1)
        for h in range(n_heads):
            w_sc[h] = jnp.where(c <= r, ws_ref[h], 0.0).astype(BF16)
        gmn1[...] = jnp.zeros(gmn1.shape, BF16)

    def gate_values(cc):
        for rows in _row_groups(CHUNK, zv_ref.shape[1], cc * CHUNK):
            gv = _gelu(zv_ref[rows, :].astype(F32))
            mu = jnp.mean(gv, axis=-1, keepdims=True)
            xc = gv - mu
            var = jnp.mean(xc * xc, axis=-1, keepdims=True)
            v_sc[rows, :] = (xc * lax.rsqrt(var + EPS) * lng_ref[...]
                             + lnb_ref[...]).astype(BF16)

    def gate_mix(cc, gmn_w):
        rows = slice(cc * CHUNK, (cc + 1) * CHUNK)
        for h in range(n_heads):
            cols = slice(h * HEAD_DIM, (h + 1) * HEAD_DIM)
            mix = jnp.dot(w_sc[h], v_sc[rows, cols], preferred_element_type=F32) + bs_ref[h]
            gm_sc[rows, cols] = _gelu(zu_ref[rows, cols].astype(F32)) * mix
        for sl in _row_groups(CHUNK, gm_sc.shape[1], cc * CHUNK):
            gmn_w[sl, :] = _rms(gm_sc[sl, :], og_ref[...]).astype(BF16)

    def step(gmn_w, gmn_r):
        for sl in _row_groups(tm, d_attn):
            an_sc[sl, :] = _rms(a_ref[sl, :].astype(F32), ag_ref[...]).astype(BF16)
        n_pieces = 2 * (tm // CHUNK)
        ncol = o_ref.shape[1] // n_pieces
        for k in range(n_pieces):
            cols = slice(k * ncol, (k + 1) * ncol)
            y = (jnp.dot(an_sc[...], w_ref[:d_attn, cols], preferred_element_type=F32)
                 + jnp.dot(gmn_r[...], w_ref[d_attn:, cols], preferred_element_type=F32))
            o_ref[:, cols] = x_ref[:, cols] + y
            if k % 2 == 0:
                gate_values(k // 2)
            else:
                gate_mix(k // 2, gmn_w)

    @pl.when(s % 2 == 0)
    def _():
        step(gmn0, gmn1)

    @pl.when(s % 2 == 1)
    def _():
        step(gmn1, gmn0)


def _mixout(z, attn, x2d, ln_g, ln_b, w_s, b_s3, gm_g, attn_g, w_out, *, d_attn, d_gmlp):
    t, d = x2d.shape
    n_heads = w_s.shape[0]
    ucol = 3 * d_attn // d_gmlp
    n = t // TM_OUT
    gate_blk = lambda s: jnp.minimum(s, n - 1)
    proj_blk = lambda s: jnp.maximum(s - 1, 0)
    return pl.pallas_call(
        _mixout_kernel,
        grid=(n + 1,),
        in_specs=[
            pl.BlockSpec((TM_OUT, d_gmlp), lambda s: (gate_blk(s), ucol)),
            pl.BlockSpec((TM_OUT, d_gmlp), lambda s: (gate_blk(s), ucol + 1)),
            pl.BlockSpec((TM_OUT, d_attn), lambda s: (proj_blk(s), 0)),
            pl.BlockSpec((TM_OUT, d), lambda s: (proj_blk(s), 0)),
            pl.BlockSpec((1, d_gmlp), lambda i: (0, 0)),
            pl.BlockSpec((1, d_gmlp), lambda i: (0, 0)),
            pl.BlockSpec((n_heads, CHUNK, CHUNK), lambda i: (0, 0, 0)),
            pl.BlockSpec((n_heads, CHUNK, 1), lambda i: (0, 0, 0)),
            pl.BlockSpec((1, d_gmlp), lambda i: (0, 0)),
            pl.BlockSpec((1, d_attn), lambda i: (0, 0)),
            pl.BlockSpec((d_attn + d_gmlp, d), lambda i: (0, 0)),
        ],
        out_specs=pl.BlockSpec((TM_OUT, d), lambda s: (proj_blk(s), 0)),
        out_shape=jax.ShapeDtypeStruct((t, d), F32),
        scratch_shapes=[
            pltpu.VMEM((n_heads, CHUNK, CHUNK), BF16),
            pltpu.VMEM((TM_OUT, d_gmlp), BF16),
            pltpu.VMEM((TM_OUT, d_gmlp), F32),
            pltpu.VMEM((TM_OUT, d_gmlp), BF16),
            pltpu.VMEM((TM_OUT, d_gmlp), BF16),
            pltpu.VMEM((TM_OUT, d_attn), BF16),
        ],
        compiler_params=_params("arbitrary"),
        name="mixout",
    )(z, z, attn, x2d, ln_g, ln_b, w_s, b_s3, gm_g, attn_g, w_out)


def _ffn_kernel(x_ref, g_ref, w1_ref, w2_ref, gf_ref, o_ref, h_ref, *, final_norm):
    f = pl.program_id(1)
    last = pl.num_programs(1) - 1
    tm = x_ref.shape[0]

    def hidden():
        a = jnp.maximum(jnp.dot(h_ref[...], w1_ref[...], preferred_element_type=F32), 0.0)
        return (a * a).astype(BF16)

    def mlp():
        return jnp.dot(hidden(), w2_ref[...], preferred_element_type=F32)

    @pl.when(f == 0)
    def _():
        for sl in _row_groups(tm, x_ref.shape[1]):
            h_ref[sl, :] = _rms(x_ref[sl, :], g_ref[...]).astype(BF16)
        o_ref[...] = mlp()

    @pl.when(jnp.logical_and(f > 0, f < last))
    def _():
        o_ref[...] += mlp()

    @pl.when(f == last)
    def _():
        a = hidden()
        part = tm // FFN_LAST_PARTS
        for r0 in range(0, tm, part):
            rows = slice(r0, r0 + part)
            y = x_ref[rows, :] + (o_ref[rows, :] + jnp.dot(
                a[rows, :], w2_ref[...], preferred_element_type=F32))
            if final_norm:
                for sl in _row_groups(part, y.shape[1]):
                    o_ref[r0 + sl.start:r0 + sl.stop, :] = _rms(y[sl, :], gf_ref[...])
            else:
                o_ref[rows, :] = y


def _ffn(x2d, g, w1, w2, g_final, *, final_norm):
    t, d = x2d.shape
    d_ff = w1.shape[1]
    assert d_ff // TF_FFN >= 2
    return pl.pallas_call(
        functools.partial(_ffn_kernel, final_norm=final_norm),
        grid=(t // TM_FFN, d_ff // TF_FFN),
        in_specs=[
            pl.BlockSpec((TM_FFN, d), lambda i, f: (i, 0)),
            pl.BlockSpec((1, d), lambda i, f: (0, 0)),
            pl.BlockSpec((d, TF_FFN), lambda i, f: (0, f)),
            pl.BlockSpec((TF_FFN, d), lambda i, f: (f, 0)),
            pl.BlockSpec((1, d), lambda i, f: (0, 0)),
        ],
        out_specs=pl.BlockSpec((TM_FFN, d), lambda i, f: (i, 0)),
        out_shape=jax.ShapeDtypeStruct((t, d), F32),
        scratch_shapes=[pltpu.VMEM((TM_FFN, d), BF16)],
        compiler_params=_params("parallel", "arbitrary"),
        name="ffn",
    )(x2d, g, w1, w2, g_final)


def kernel(x, norm_mix_g, w_in, b_f, gmlp_ln_g, gmlp_ln_b, w_s, b_s, attn_out_g, gmlp_out_g,
           w_out, norm_ffn_g, w_ff1, w_ff2, norm_final_g):
    batch, seq, d = x.shape
    depth = w_in.shape[0]
    n_heads = b_f.shape[1]
    d_attn = n_heads * HEAD_DIM
    d_gmlp = gmlp_ln_g.shape[1]
    o3 = 3 * d_attn
    o4 = o3 + n_heads
    assert d_attn == d_gmlp and w_in.shape[2] == o4 + 2 * d_gmlp

    x2d = x.reshape(batch * seq, d)
    for l in range(depth):
        w_main, w_gate = _wprep(jnp.swapaxes(w_in[l], 0, 1), o3=o3, n_gate=n_heads)
        w_f = jnp.swapaxes(_decay_lanes(jnp.swapaxes(w_gate, 0, 1), n_heads), 0, 1).astype(BF16)
        bf_row = _decay_lanes(b_f[l].astype(F32), n_heads).reshape(1, LANES)

        z, zf = _inproj(x2d, norm_mix_g[l].reshape(1, d), w_main, w_f, d_attn=d_attn)
        qa, ka = _decay(zf, bf_row, batch=batch, seq=seq, n_heads=n_heads)
        attn, (w_out16, w_ff1_16, w_ff2_16) = _attention(
            z, qa, ka, [w_out[l], w_ff1[l], w_ff2[l]], batch=batch, seq=seq, n_heads=n_heads)
        x2d = _mixout(z, attn, x2d, gmlp_ln_g[l].reshape(1, d_gmlp),
                      gmlp_ln_b[l].reshape(1, d_gmlp), w_s[l], b_s[l].reshape(n_heads, CHUNK, 1),
                      gmlp_out_g[l].reshape(1, d_gmlp), attn_out_g[l].reshape(1, d_attn),
                      w_out16, d_attn=d_attn, d_gmlp=d_gmlp)
        x2d = _ffn(x2d, norm_ffn_g[l].reshape(1, d), w_ff1_16, w_ff2_16,
                   norm_final_g.reshape(1, d), final_norm=(l == depth - 1))
    return x2d.reshape(batch, seq, d)
```

```python
import functools
import math

import jax
import jax.numpy as jnp
from jax import lax
from jax.experimental import pallas as pl
from jax.experimental.pallas import tpu as pltpu

F32 = jnp.float32
BF16 = jnp.bfloat16

HEAD_DIM = 128
CHUNK = 128
EPS = 1e-6
LANES = 128
LOG2E = 1.4426950408889634
VMEM_LIMIT = 62 * 1024 * 1024

TM_WPREP = 512
TM_PROJ = 1024
TN_PROJ = 2560
TM_DECAY = 2048
TQ = 512
TK = 512
ATTN_SLOTS = 4
TM_OUT = 512
MIX_PARTS = 2
TM_FFN = 1024
TF_FFN = 1024
FFN_LAST_PARTS = 4
NORM_VREGS = 32
CUM_BLK = 256
DECAY_LANES = 8


def _row_groups(n_rows, n_cols, start=0):
    step = max(16, NORM_VREGS * 8 * LANES // n_cols)
    assert n_rows % step == 0
    return [slice(start + r, start + r + step) for r in range(0, n_rows, step)]


def _rms(x, g):
    ms = jnp.mean(x * x, axis=-1, keepdims=True)
    return x * lax.rsqrt(ms + EPS) * g


def _gelu(x):
    c = math.sqrt(2.0 / math.pi)
    inner = x * (c + (c * 0.044715) * (x * x))
    hx = 0.5 * x
    return hx + hx * jnp.tanh(inner)


def _split3(x):
    hi = x.astype(BF16)
    r1 = x - hi.astype(F32)
    mid = r1.astype(BF16)
    lo = (r1 - mid.astype(F32)).astype(BF16)
    return hi, mid, lo


def _params(*sem):
    return pltpu.CompilerParams(dimension_semantics=sem, vmem_limit_bytes=VMEM_LIMIT)


def _wprep_kernel(a_ref, b_ref, main_ref, gate_ref, *, gate_block, n_gate):
    j = pl.program_id(0)

    @pl.when(j < gate_block)
    def _():
        main_ref[...] = a_ref[...].astype(BF16)

    @pl.when(j >= gate_block)
    def _():
        main_ref[...] = jnp.concatenate([a_ref[n_gate:, :], b_ref[...]], axis=0).astype(BF16)

    @pl.when(j == gate_block)
    def _():
        gate_ref[...] = a_ref[:n_gate, :]


def _wprep(wt, *, o3, n_gate):
    n, d = wt.shape
    assert o3 % TM_WPREP == 0 and (n - n_gate) % TM_WPREP == 0 and TM_WPREP % n_gate == 0
    assert n_gate % 8 == 0
    per = TM_WPREP // n_gate
    return pl.pallas_call(
        functools.partial(_wprep_kernel, gate_block=o3 // TM_WPREP, n_gate=n_gate),
        grid=((n - n_gate) // TM_WPREP,),
        in_specs=[
            pl.BlockSpec((TM_WPREP, d), lambda j: (j, 0)),
            pl.BlockSpec((n_gate, d), lambda j: ((j + 1) * per, 0)),
        ],
        out_specs=[
            pl.BlockSpec((TM_WPREP, d), lambda j: (j, 0)),
            pl.BlockSpec((n_gate, d), lambda j: (0, 0)),
        ],
        out_shape=[
            jax.ShapeDtypeStruct((n - n_gate, d), BF16),
            jax.ShapeDtypeStruct((n_gate, d), F32),
        ],
        compiler_params=_params("arbitrary"),
        name="wprep",
    )(wt, wt)


def _inproj_kernel(x_ref, g_ref, w_ref, wf_ref, z_ref, zf_ref, h_ref, *, d_q, q_scale):
    j = pl.program_id(1)
    tm = x_ref.shape[0]

    nt = (((1,), (1,)), ((), ()))

    @pl.when(j == 0)
    def _():
        for sl in _row_groups(tm, x_ref.shape[1]):
            h_ref[sl, :] = _rms(x_ref[sl, :], g_ref[...]).astype(BF16)
        zf_ref[...] = lax.dot_general(h_ref[...], wf_ref[...], nt, preferred_element_type=F32)
        q = lax.dot_general(h_ref[...], w_ref[:d_q, :], nt, preferred_element_type=F32)
        z_ref[:, :d_q] = (q * q_scale).astype(BF16)
        z_ref[:, d_q:] = lax.dot_general(h_ref[...], w_ref[d_q:, :], nt,
                                         preferred_element_type=F32).astype(BF16)

    @pl.when(j > 0)
    def _():
        z_ref[...] = lax.dot_general(h_ref[...], w_ref[...], nt,
                                     preferred_element_type=F32).astype(BF16)


def _inproj(x2d, g, w_main, w_f, *, d_attn):
    t, d = x2d.shape
    n = w_main.shape[0]
    q_scale = LOG2E / math.sqrt(HEAD_DIM)
    assert d_attn < TN_PROJ and d_attn % LANES == 0
    return pl.pallas_call(
        functools.partial(_inproj_kernel, d_q=d_attn, q_scale=q_scale),
        grid=(t // TM_PROJ, n // TN_PROJ),
        in_specs=[
            pl.BlockSpec((TM_PROJ, d), lambda i, j: (i, 0)),
            pl.BlockSpec((1, d), lambda i, j: (0, 0)),
            pl.BlockSpec((TN_PROJ, d), lambda i, j: (j, 0)),
            pl.BlockSpec((LANES, d), lambda i, j: (0, 0)),
        ],
        out_specs=[
            pl.BlockSpec((TM_PROJ, TN_PROJ), lambda i, j: (i, j)),
            pl.BlockSpec((TM_PROJ, LANES), lambda i, j: (i, 0)),
        ],
        out_shape=[
            jax.ShapeDtypeStruct((t, n), BF16),
            jax.ShapeDtypeStruct((t, LANES), F32),
        ],
        scratch_shapes=[pltpu.VMEM((TM_PROJ, d), BF16)],
        compiler_params=_params("parallel", "arbitrary"),
        name="inproj",
    )(x2d, g, w_main, w_f)


def _decay_kernel(zf_ref, bf_ref, qa_ref, ka_ref, f_ref, carry_ref, *, n_heads):
    rows = zf_ref.shape[0]

    @pl.when(pl.program_id(1) == 0)
    def _():
        carry_ref[...] = jnp.zeros(carry_ref.shape, F32)

    row = lax.broadcasted_iota(jnp.int32, (CUM_BLK, CUM_BLK), 0)
    col = lax.broadcasted_iota(jnp.int32, (CUM_BLK, CUM_BLK), 1)
    tri = (col <= row).astype(BF16)
    carry = carry_ref[...]
    for r in range(rows // CUM_BLK):
        sl = slice(r * CUM_BLK, (r + 1) * CUM_BLK)
        x = zf_ref[sl, :] + bf_ref[...]
        logf = jnp.minimum(x, 0.0) - jnp.log1p(jnp.exp(-jnp.abs(x)))
        hi, mid, lo = _split3(logf)
        c = (jnp.dot(tri, hi, preferred_element_type=F32)
             + jnp.dot(tri, mid, preferred_element_type=F32)
             + jnp.dot(tri, lo, preferred_element_type=F32)) + carry
        f_ref[sl, :] = c
        carry = c[CUM_BLK - 1:CUM_BLK, :]
    carry_ref[...] = carry

    hi, mid, lo = _split3(f_ref[...] * LOG2E)
    lane = lax.broadcasted_iota(jnp.int32, (rows, LANES), 1)
    sub = lane & (DECAY_LANES - 1)
    used = lane < n_heads * DECAY_LANES
    piece_idx = jnp.where(sub >= 3, sub - 3, sub)
    piece = jnp.where(piece_idx == 0, hi, jnp.where(piece_idx == 1, mid, lo)).astype(F32)
    first = jnp.logical_and(used, sub < 3)
    second = jnp.logical_and(used, jnp.logical_and(sub >= 3, sub < 6))
    qa_ref[...] = jnp.where(first, piece, jnp.where(second, 1.0, 0.0)).astype(BF16)
    ka_ref[...] = jnp.where(first, 1.0, jnp.where(second, -piece, 0.0)).astype(BF16)


def _decay_lanes(v, n_heads):
    lead = v.shape[:-1]
    rep = jnp.broadcast_to(v[..., None], lead + (n_heads, 6))
    rep = jnp.pad(rep, [(0, 0)] * len(lead) + [(0, 0), (0, DECAY_LANES - 6)])
    rep = rep.reshape(lead + (n_heads * DECAY_LANES,))
    return jnp.pad(rep, [(0, 0)] * len(lead) + [(0, LANES - n_heads * DECAY_LANES)])


def _decay(zf, bf_row, *, batch, seq, n_heads):
    assert n_heads * DECAY_LANES <= LANES
    ns = seq // TM_DECAY
    return pl.pallas_call(
        functools.partial(_decay_kernel, n_heads=n_heads),
        grid=(batch, ns),
        in_specs=[
            pl.BlockSpec((TM_DECAY, LANES), lambda b, s: (b * ns + s, 0)),
            pl.BlockSpec((1, LANES), lambda b, s: (0, 0)),
        ],
        out_specs=[
            pl.BlockSpec((TM_DECAY, LANES), lambda b, s: (b * ns + s, 0)),
            pl.BlockSpec((TM_DECAY, LANES), lambda b, s: (b * ns + s, 0)),
        ],
        out_shape=[
            jax.ShapeDtypeStruct((batch * seq, LANES), BF16),
            jax.ShapeDtypeStruct((batch * seq, LANES), BF16),
        ],
        scratch_shapes=[pltpu.VMEM((TM_DECAY, LANES), F32), pltpu.VMEM((1, LANES), F32)],
        compiler_params=_params("parallel", "arbitrary"),
        name="decay",
    )(zf, bf_row)


def _attn_kernel(q_ref, qa_ref, k_ref, ka_ref, v_ref, *rest, pairs, n_cast):
    cast_in, rest = rest[:n_cast], rest[n_cast:]
    o_ref, rest = rest[0], rest[1:]
    cast_out, rest = rest[:n_cast], rest[n_cast:]
    for src, dst in zip(cast_in, cast_out):
        dst[...] = src[...].astype(BF16)
    _attn_body(q_ref, qa_ref, k_ref, ka_ref, v_ref, o_ref, *rest, pairs=pairs)


def _attn_body(q_ref, qa_ref, k_ref, ka_ref, v_ref, o_ref,
               qam_ref, m_ref, acc_ref, s_ref, mn_ref, a_ref, *, pairs):
    lane = lax.broadcasted_iota(jnp.int32, qa_ref.shape, 1)
    head_lane0 = pl.program_id(1) * DECAY_LANES
    mine = jnp.logical_and(lane >= head_lane0, lane < head_lane0 + DECAY_LANES)
    qam_ref[...] = jnp.where(mine, qa_ref[...], jnp.zeros(qa_ref.shape, BF16))
    ones = jnp.ones((TK, HEAD_DIM), BF16)

    def rows(blk, size):
        return slice(blk * size, (blk + 1) * size)

    def stage1(g):
        qi, kj = pairs[g]
        qr, kr = rows(qi, TQ), rows(kj, TK)
        qp = jnp.concatenate([q_ref[qr, :], qam_ref[qr, :]], axis=1)
        kp = jnp.concatenate([k_ref[kr, :], ka_ref[kr, :]], axis=1)
        s = lax.dot_general(qp, kp, (((1,), (1,)), ((), ())), preferred_element_type=F32)
        if qi == kj:
            r = lax.broadcasted_iota(jnp.int32, (TQ, TK), 0)
            c = lax.broadcasted_iota(jnp.int32, (TQ, TK), 1)
            s = jnp.where(c <= r, s, -jnp.inf)
        s_ref[g % ATTN_SLOTS] = s

    def stage2(g):
        qi, _ = pairs[g]
        qr, slot = rows(qi, TQ), g % ATTN_SLOTS
        bm = jnp.max(s_ref[slot], axis=1, keepdims=True)
        if first[g]:
            m_new = jnp.broadcast_to(bm, (TQ, LANES))
        else:
            m_prev = m_ref[qr, :]
            m_new = jnp.maximum(m_prev, bm)
            a_ref[slot] = jnp.exp2(m_prev - m_new)
        mn_ref[slot] = m_new
        m_ref[qr, :] = m_new

    def stage3(g):
        qi, kj = pairs[g]
        qr, kr, slot = rows(qi, TQ), rows(kj, TK), g % ATTN_SLOTS
        m_new = mn_ref[slot]
        p = jnp.exp2(s_ref[slot] - jnp.concatenate([m_new] * (TK // LANES), axis=1))
        vp = jnp.concatenate([v_ref[kr, :], ones], axis=1)
        pv = jnp.dot(p.astype(BF16), vp, preferred_element_type=F32)
        if first[g]:
            acc = pv
        else:
            alpha = a_ref[slot]
            acc = acc_ref[qr, :] * jnp.concatenate([alpha, alpha], axis=1) + pv
        if qi == kj:
            o_ref[qr, :] = (acc[:, :HEAD_DIM] / acc[:, HEAD_DIM:]).astype(BF16)
        else:
            acc_ref[qr, :] = acc

    seen = set()
    first = []
    for qi, _ in pairs:
        first.append(qi not in seen)
        seen.add(qi)
    total = len(pairs)
    for g in range(total + 2):
        if g >= 2:
            stage3(g - 2)
        if 1 <= g <= total:
            stage2(g - 1)
        if g < total:
            stage1(g)


def _attention(z, qa, ka, cast_weights, *, batch, seq, n_heads):
    assert TQ == TK and ATTN_SLOTS >= 3
    nq = seq // TQ
    t = batch * seq
    steps = batch * n_heads
    cast_specs = [pl.BlockSpec((w.shape[0] // steps, w.shape[1]),
                               lambda b, h: (b * n_heads + h, 0)) for w in cast_weights]
    pairs = tuple([(i, j) for i in range(nq) for j in range(i)] + [(i, i) for i in range(nq)])
    outs = pl.pallas_call(
        functools.partial(_attn_kernel, pairs=pairs, n_cast=len(cast_weights)),
        grid=(batch, n_heads),
        in_specs=[
            pl.BlockSpec((seq, HEAD_DIM), lambda b, h: (b, h)),
            pl.BlockSpec((seq, LANES), lambda b, h: (b, 0)),
            pl.BlockSpec((seq, HEAD_DIM), lambda b, h: (b, n_heads + h)),
            pl.BlockSpec((seq, LANES), lambda b, h: (b, 0)),
            pl.BlockSpec((seq, HEAD_DIM), lambda b, h: (b, 2 * n_heads + h)),
        ] + cast_specs,
        out_specs=[pl.BlockSpec((seq, HEAD_DIM), lambda b, h: (b, h))] + cast_specs,
        out_shape=[jax.ShapeDtypeStruct((t, n_heads * HEAD_DIM), BF16)]
        + [jax.ShapeDtypeStruct(w.shape, BF16) for w in cast_weights],
        scratch_shapes=[
            pltpu.VMEM((seq, LANES), BF16),
            pltpu.VMEM((seq, LANES), F32),
            pltpu.VMEM((seq, 2 * HEAD_DIM), F32),
            pltpu.VMEM((ATTN_SLOTS, TQ, TK), F32),
            pltpu.VMEM((ATTN_SLOTS, TQ, LANES), F32),
            pltpu.VMEM((ATTN_SLOTS, TQ, LANES), F32),
        ],
        compiler_params=_params("parallel", "parallel"),
        name="attention",
    )(z, qa, z, ka, z, *cast_weights)
    return outs[0], outs[1:]


def _mixout_kernel(zu_ref, zv_ref, a_ref, x_ref, lng_ref, lnb_ref, ws_ref, bs_ref, og_ref,
                   ag_ref, w_ref, o_ref, w_sc, v_sc, gm_sc, gmn0, gmn1, an_sc):
    s = pl.program_id(0)
    tm = zu_ref.shape[0]
    n_heads = ws_ref.shape[0]
    d_attn = a_ref.shape[1]

    @pl.when(s == 0)
    def _():
        r = lax.broadcasted_iota(jnp.int32, (CHUNK, CHUNK), 0)
        c = lax.broadcasted_iota(jnp.int32, (CHUNK, CHUNK), 1)
        for h in range(n_heads):
            w_sc[h] = jnp.where(c <= r, ws_ref[h], 0.0).astype(BF16)
        gmn1[...] = jnp.zeros(gmn1.shape, BF16)

    def gate_values(cc, part):
        groups = _row_groups(CHUNK, zv_ref.shape[1], cc * CHUNK)
        for rows in groups[part * len(groups) // MIX_PARTS:(part + 1) * len(groups) // MIX_PARTS]:
            gv = _gelu(zv_ref[rows, :].astype(F32))
            mu = jnp.mean(gv, axis=-1, keepdims=True)
            xc = gv - mu
            var = jnp.mean(xc * xc, axis=-1, keepdims=True)
            v_sc[rows, :] = (xc * lax.rsqrt(var + EPS) * lng_ref[...]
                             + lnb_ref[...]).astype(BF16)

    def gate_mix(cc, part, gmn_w):
        rows = slice(cc * CHUNK, (cc + 1) * CHUNK)
        for h in range(part * n_heads // MIX_PARTS, (part + 1) * n_heads // MIX_PARTS):
            cols = slice(h * HEAD_DIM, (h + 1) * HEAD_DIM)
            mix = jnp.dot(w_sc[h], v_sc[rows, cols], preferred_element_type=F32) + bs_ref[h]
            gm_sc[rows, cols] = _gelu(zu_ref[rows, cols].astype(F32)) * mix
        if part == MIX_PARTS - 1:
            for sl in _row_groups(CHUNK, gm_sc.shape[1], cc * CHUNK):
                gmn_w[sl, :] = _rms(gm_sc[sl, :], og_ref[...]).astype(BF16)

    def step(gmn_w, gmn_r):
        for sl in _row_groups(tm, d_attn):
            an_sc[sl, :] = _rms(a_ref[sl, :].astype(F32), ag_ref[...]).astype(BF16)
        n_slices = 2 * (tm // CHUNK)
        ncol = o_ref.shape[1] // n_slices
        kpart = w_ref.shape[0] // MIX_PARTS
        lhs = jnp.concatenate([an_sc[...], gmn_r[...]], axis=1)
        for k in range(n_slices):
            cols = slice(k * ncol, (k + 1) * ncol)
            for part in range(MIX_PARTS):
                ks = slice(part * kpart, (part + 1) * kpart)
                y = jnp.dot(lhs[:, ks], w_ref[ks, cols], preferred_element_type=F32)
                if part == 0:
                    o_ref[:, cols] = x_ref[:, cols] + y
                else:
                    o_ref[:, cols] += y
                if k % 2 == 0:
                    gate_values(k // 2, part)
                else:
                    gate_mix(k // 2, part, gmn_w)

    @pl.when(s % 2 == 0)
    def _():
        step(gmn0, gmn1)

    @pl.when(s % 2 == 1)
    def _():
        step(gmn1, gmn0)


def _mixout(z, attn, x2d, ln_g, ln_b, w_s, b_s3, gm_g, attn_g, w_out, *, d_attn, d_gmlp):
    t, d = x2d.shape
    n_heads = w_s.shape[0]
    ucol = 3 * d_attn // d_gmlp
    n = t // TM_OUT
    gate_blk = lambda s: jnp.minimum(s, n - 1)
    proj_blk = lambda s: jnp.maximum(s - 1, 0)
    return pl.pallas_call(
        _mixout_kernel,
        grid=(n + 1,),
        in_specs=[
            pl.BlockSpec((TM_OUT, d_gmlp), lambda s: (gate_blk(s), ucol)),
            pl.BlockSpec((TM_OUT, d_gmlp), lambda s: (gate_blk(s), ucol + 1)),
            pl.BlockSpec((TM_OUT, d_attn), lambda s: (proj_blk(s), 0)),
            pl.BlockSpec((TM_OUT, d), lambda s: (proj_blk(s), 0)),
            pl.BlockSpec((1, d_gmlp), lambda i: (0, 0)),
            pl.BlockSpec((1, d_gmlp), lambda i: (0, 0)),
            pl.BlockSpec((n_heads, CHUNK, CHUNK), lambda i: (0, 0, 0)),
            pl.BlockSpec((n_heads, CHUNK, 1), lambda i: (0, 0, 0)),
            pl.BlockSpec((1, d_gmlp), lambda i: (0, 0)),
            pl.BlockSpec((1, d_attn), lambda i: (0, 0)),
            pl.BlockSpec((d_attn + d_gmlp, d), lambda i: (0, 0)),
        ],
        out_specs=pl.BlockSpec((TM_OUT, d), lambda s: (proj_blk(s), 0)),
        out_shape=jax.ShapeDtypeStruct((t, d), F32),
        scratch_shapes=[
            pltpu.VMEM((n_heads, CHUNK, CHUNK), BF16),
            pltpu.VMEM((TM_OUT, d_gmlp), BF16),
            pltpu.VMEM((TM_OUT, d_gmlp), F32),
            pltpu.VMEM((TM_OUT, d_gmlp), BF16),
            pltpu.VMEM((TM_OUT, d_gmlp), BF16),
            pltpu.VMEM((TM_OUT, d_attn), BF16),
        ],
        compiler_params=_params("arbitrary"),
        name="mixout",
    )(z, z, attn, x2d, ln_g, ln_b, w_s, b_s3, gm_g, attn_g, w_out)


def _ffn_kernel(x_ref, g_ref, w1_ref, w2_ref, gf_ref, o_ref, h_ref, *, final_norm):
    f = pl.program_id(1)
    last = pl.num_programs(1) - 1
    tm = x_ref.shape[0]

    def hidden():
        a = jnp.maximum(jnp.dot(h_ref[...], w1_ref[...], preferred_element_type=F32), 0.0)
        return (a * a).astype(BF16)

    def mlp():
        return jnp.dot(hidden(), w2_ref[...], preferred_element_type=F32)

    @pl.when(f == 0)
    def _():
        for sl in _row_groups(tm, x_ref.shape[1]):
            h_ref[sl, :] = _rms(x_ref[sl, :], g_ref[...]).astype(BF16)
        o_ref[...] = mlp()

    @pl.when(jnp.logical_and(f > 0, f < last))
    def _():
        o_ref[...] += mlp()

    @pl.when(f == last)
    def _():
        a = hidden()
        part = tm // FFN_LAST_PARTS
        for r0 in range(0, tm, part):
            rows = slice(r0, r0 + part)
            y = x_ref[rows, :] + (o_ref[rows, :] + jnp.dot(
                a[rows, :], w2_ref[...], preferred_element_type=F32))
            if final_norm:
                for sl in _row_groups(part, y.shape[1]):
                    o_ref[r0 + sl.start:r0 + sl.stop, :] = _rms(y[sl, :], gf_ref[...])
            else:
                o_ref[rows, :] = y


def _ffn(x2d, g, w1, w2, g_final, *, final_norm):
    t, d = x2d.shape
    d_ff = w1.shape[1]
    assert d_ff // TF_FFN >= 2
    return pl.pallas_call(
        functools.partial(_ffn_kernel, final_norm=final_norm),
        grid=(t // TM_FFN, d_ff // TF_FFN),
        in_specs=[
            pl.BlockSpec((TM_FFN, d), lambda i, f: (i, 0)),
            pl.BlockSpec((1, d), lambda i, f: (0, 0)),
            pl.BlockSpec((d, TF_FFN), lambda i, f: (0, f)),
            pl.BlockSpec((TF_FFN, d), lambda i, f: (f, 0)),
            pl.BlockSpec((1, d), lambda i, f: (0, 0)),
        ],
        out_specs=pl.BlockSpec((TM_FFN, d), lambda i, f: (i, 0)),
        out_shape=jax.ShapeDtypeStruct((t, d), F32),
        scratch_shapes=[pltpu.VMEM((TM_FFN, d), BF16)],
        compiler_params=_params("parallel", "arbitrary"),
        name="ffn",
    )(x2d, g, w1, w2, g_final)


def kernel(x, norm_mix_g, w_in, b_f, gmlp_ln_g, gmlp_ln_b, w_s, b_s, attn_out_g, gmlp_out_g,
           w_out, norm_ffn_g, w_ff1, w_ff2, norm_final_g):
    batch, seq, d = x.shape
    depth = w_in.shape[0]
    n_heads = b_f.shape[1]
    d_attn = n_heads * HEAD_DIM
    d_gmlp = gmlp_ln_g.shape[1]
    o3 = 3 * d_attn
    o4 = o3 + n_heads
    assert d_attn == d_gmlp and w_in.shape[2] == o4 + 2 * d_gmlp

    x2d = x.reshape(batch * seq, d)
    for l in range(depth):
        w_main, w_gate = _wprep(jnp.swapaxes(w_in[l], 0, 1), o3=o3, n_gate=n_heads)
        w_f = jnp.swapaxes(_decay_lanes(jnp.swapaxes(w_gate, 0, 1), n_heads), 0, 1).astype(BF16)
        bf_row = _decay_lanes(b_f[l].astype(F32), n_heads).reshape(1, LANES)

        z, zf = _inproj(x2d, norm_mix_g[l].reshape(1, d), w_main, w_f, d_attn=d_attn)
        qa, ka = _decay(zf, bf_row, batch=batch, seq=seq, n_heads=n_heads)
        attn, (w_out16, w_ff1_16, w_ff2_16) = _attention(
            z, qa, ka, [w_out[l], w_ff1[l], w_ff2[l]], batch=batch, seq=seq, n_heads=n_heads)
        x2d = _mixout(z, attn, x2d, gmlp_ln_g[l].reshape(1, d_gmlp),
                      gmlp_ln_b[l].reshape(1, d_gmlp), w_s[l], b_s[l].reshape(n_heads, CHUNK, 1),
                      gmlp_out_g[l].reshape(1, d_gmlp), attn_out_g[l].reshape(1, d_attn),
                      w_out16, d_attn=d_attn, d_gmlp=d_gmlp)
        x2d = _ffn(x2d, norm_ffn_g[l].reshape(1, d), w_ff1_16, w_ff2_16,
                   norm_final_g.reshape(1, d), final_norm=(l == depth - 1))
    return x2d.reshape(batch, seq, d)
```

```python
import functools
import math

import jax
import jax.numpy as jnp
from jax import lax
from jax.experimental import pallas as pl
from jax.experimental.pallas import tpu as pltpu

F32 = jnp.float32
BF16 = jnp.bfloat16

HEAD_DIM = 128
CHUNK = 128
EPS = 1e-6
LANES = 128
LOG2E = 1.4426950408889634
VMEM_LIMIT = 62 * 1024 * 1024

TM_WPREP = 512
TM_PROJ = 1024
TN_PROJ = 2560
TM_DECAY = 2048
TQ = 512
TK = 512
ATTN_SLOTS = 4
TM_OUT = 512
MIX_PARTS = 2
TM_FFN = 1024
TF_FFN = 1024
FFN_LAST_PARTS = 4
NORM_VREGS = 32
CUM_BLK = 256
DECAY_LANES = 8


def _row_groups(n_rows, n_cols, start=0):
    step = max(16, NORM_VREGS * 8 * LANES // n_cols)
    assert n_rows % step == 0
    return [slice(start + r, start + r + step) for r in range(0, n_rows, step)]


def _rms(x, g):
    ms = jnp.mean(x * x, axis=-1, keepdims=True)
    return x * lax.rsqrt(ms + EPS) * g


def _gelu(x):
    c = math.sqrt(2.0 / math.pi)
    inner = x * (c + (c * 0.044715) * (x * x))
    hx = 0.5 * x
    return hx + hx * jnp.tanh(inner)


def _split3(x):
    hi = x.astype(BF16)
    r1 = x - hi.astype(F32)
    mid = r1.astype(BF16)
    lo = (r1 - mid.astype(F32)).astype(BF16)
    return hi, mid, lo


def _params(*sem):
    return pltpu.CompilerParams(dimension_semantics=sem, vmem_limit_bytes=VMEM_LIMIT)


def _wprep_kernel(a_ref, b_ref, main_ref, gate_ref, *, gate_block, n_gate):
    j = pl.program_id(0)

    @pl.when(j < gate_block)
    def _():
        main_ref[...] = a_ref[...].astype(BF16)

    @pl.when(j >= gate_block)
    def _():
        main_ref[...] = jnp.concatenate([a_ref[n_gate:, :], b_ref[...]], axis=0).astype(BF16)

    @pl.when(j == gate_block)
    def _():
        gate_ref[...] = a_ref[:n_gate, :]


def _wprep(wt, *, o3, n_gate):
    n, d = wt.shape
    assert o3 % TM_WPREP == 0 and (n - n_gate) % TM_WPREP == 0 and TM_WPREP % n_gate == 0
    assert n_gate % 8 == 0
    per = TM_WPREP // n_gate
    return pl.pallas_call(
        functools.partial(_wprep_kernel, gate_block=o3 // TM_WPREP, n_gate=n_gate),
        grid=((n - n_gate) // TM_WPREP,),
        in_specs=[
            pl.BlockSpec((TM_WPREP, d), lambda j: (j, 0)),
            pl.BlockSpec((n_gate, d), lambda j: ((j + 1) * per, 0)),
        ],
        out_specs=[
            pl.BlockSpec((TM_WPREP, d), lambda j: (j, 0)),
            pl.BlockSpec((n_gate, d), lambda j: (0, 0)),
        ],
        out_shape=[
            jax.ShapeDtypeStruct((n - n_gate, d), BF16),
            jax.ShapeDtypeStruct((n_gate, d), F32),
        ],
        compiler_params=_params("arbitrary"),
        name="wprep",
    )(wt, wt)


def _inproj_kernel(x_ref, g_ref, w_ref, wf_ref, z_ref, zf_ref, h_ref, *, d_q, q_scale):
    j = pl.program_id(1)
    tm = x_ref.shape[0]

    nt = (((1,), (1,)), ((), ()))

    @pl.when(j == 0)
    def _():
        for sl in _row_groups(tm, x_ref.shape[1]):
            h_ref[sl, :] = _rms(x_ref[sl, :], g_ref[...]).astype(BF16)
        zf_ref[...] = lax.dot_general(h_ref[...], wf_ref[...], nt, preferred_element_type=F32)
        q = lax.dot_general(h_ref[...], w_ref[:d_q, :], nt, preferred_element_type=F32)
        z_ref[:, :d_q] = (q * q_scale).astype(BF16)
        z_ref[:, d_q:] = lax.dot_general(h_ref[...], w_ref[d_q:, :], nt,
                                         preferred_element_type=F32).astype(BF16)

    @pl.when(j > 0)
    def _():
        z_ref[...] = lax.dot_general(h_ref[...], w_ref[...], nt,
                                     preferred_element_type=F32).astype(BF16)


def _inproj(x2d, g, w_main, w_f, *, d_attn):
    t, d = x2d.shape
    n = w_main.shape[0]
    q_scale = LOG2E / math.sqrt(HEAD_DIM)
    assert d_attn < TN_PROJ and d_attn % LANES == 0
    return pl.pallas_call(
        functools.partial(_inproj_kernel, d_q=d_attn, q_scale=q_scale),
        grid=(t // TM_PROJ, n // TN_PROJ),
        in_specs=[
            pl.BlockSpec((TM_PROJ, d), lambda i, j: (i, 0)),
            pl.BlockSpec((1, d), lambda i, j: (0, 0)),
            pl.BlockSpec((TN_PROJ, d), lambda i, j: (j, 0)),
            pl.BlockSpec((LANES, d), lambda i, j: (0, 0)),
        ],
        out_specs=[
            pl.BlockSpec((TM_PROJ, TN_PROJ), lambda i, j: (i, j)),
            pl.BlockSpec((TM_PROJ, LANES), lambda i, j: (i, 0)),
        ],
        out_shape=[
            jax.ShapeDtypeStruct((t, n), BF16),
            jax.ShapeDtypeStruct((t, LANES), F32),
        ],
        scratch_shapes=[pltpu.VMEM((TM_PROJ, d), BF16)],
        compiler_params=_params("parallel", "arbitrary"),
        name="inproj",
    )(x2d, g, w_main, w_f)


def _decay_kernel(zf_ref, bf_ref, qa_ref, ka_ref, f_ref, carry_ref, *, n_heads):
    rows = zf_ref.shape[0]

    @pl.when(pl.program_id(1) == 0)
    def _():
        carry_ref[...] = jnp.zeros(carry_ref.shape, F32)

    row = lax.broadcasted_iota(jnp.int32, (CUM_BLK, CUM_BLK), 0)
    col = lax.broadcasted_iota(jnp.int32, (CUM_BLK, CUM_BLK), 1)
    tri = (col <= row).astype(BF16)
    carry = carry_ref[...]
    for r in range(rows // CUM_BLK):
        sl = slice(r * CUM_BLK, (r + 1) * CUM_BLK)
        x = zf_ref[sl, :] + bf_ref[...]
        logf = jnp.minimum(x, 0.0) - jnp.log1p(jnp.exp(-jnp.abs(x)))
        hi, mid, lo = _split3(logf)
        c = (jnp.dot(tri, hi, preferred_element_type=F32)
             + jnp.dot(tri, mid, preferred_element_type=F32)
             + jnp.dot(tri, lo, preferred_element_type=F32)) + carry
        f_ref[sl, :] = c
        carry = c[CUM_BLK - 1:CUM_BLK, :]
    carry_ref[...] = carry

    hi, mid, lo = _split3(f_ref[...] * LOG2E)
    lane = lax.broadcasted_iota(jnp.int32, (rows, LANES), 1)
    sub = lane & (DECAY_LANES - 1)
    used = lane < n_heads * DECAY_LANES
    piece_idx = jnp.where(sub >= 3, sub - 3, sub)
    piece = jnp.where(piece_idx == 0, hi, jnp.where(piece_idx == 1, mid, lo)).astype(F32)
    first = jnp.logical_and(used, sub < 3)
    second = jnp.logical_and(used, jnp.logical_and(sub >= 3, sub < 6))
    qa_ref[...] = jnp.where(first, piece, jnp.where(second, 1.0, 0.0)).astype(BF16)
    ka_ref[...] = jnp.where(first, 1.0, jnp.where(second, -piece, 0.0)).astype(BF16)


def _decay_lanes(v, n_heads):
    lead = v.shape[:-1]
    rep = jnp.broadcast_to(v[..., None], lead + (n_heads, 6))
    rep = jnp.pad(rep, [(0, 0)] * len(lead) + [(0, 0), (0, DECAY_LANES - 6)])
    rep = rep.reshape(lead + (n_heads * DECAY_LANES,))
    return jnp.pad(rep, [(0, 0)] * len(lead) + [(0, LANES - n_heads * DECAY_LANES)])


def _decay(zf, bf_row, *, batch, seq, n_heads):
    assert n_heads * DECAY_LANES <= LANES
    ns = seq // TM_DECAY
    return pl.pallas_call(
        functools.partial(_decay_kernel, n_heads=n_heads),
        grid=(batch, ns),
        in_specs=[
            pl.BlockSpec((TM_DECAY, LANES), lambda b, s: (b * ns + s, 0)),
            pl.BlockSpec((1, LANES), lambda b, s: (0, 0)),
        ],
        out_specs=[
            pl.BlockSpec((TM_DECAY, LANES), lambda b, s: (b * ns + s, 0)),
            pl.BlockSpec((TM_DECAY, LANES), lambda b, s: (b * ns + s, 0)),
        ],
        out_shape=[
            jax.ShapeDtypeStruct((batch * seq, LANES), BF16),
            jax.ShapeDtypeStruct((batch * seq, LANES), BF16),
        ],
        scratch_shapes=[pltpu.VMEM((TM_DECAY, LANES), F32), pltpu.VMEM((1, LANES), F32)],
        compiler_params=_params("parallel", "arbitrary"),
        name="decay",
    )(zf, bf_row)


def _attn_kernel(q_ref, qa_ref, k_ref, ka_ref, v_ref, *rest, pairs, n_cast):
    cast_in, rest = rest[:n_cast], rest[n_cast:]
    o_ref, rest = rest[0], rest[1:]
    cast_out, rest = rest[:n_cast], rest[n_cast:]
    for src, dst in zip(cast_in, cast_out):
        dst[...] = src[...].astype(BF16)
    _attn_body(q_ref, qa_ref, k_ref, ka_ref, v_ref, o_ref, *rest, pairs=pairs)


def _attn_body(q_ref, qa_ref, k_ref, ka_ref, v_ref, o_ref,
               qam_ref, m_ref, acc_ref, s_ref, mn_ref, a_ref, *, pairs):
    lane = lax.broadcasted_iota(jnp.int32, qa_ref.shape, 1)
    head_lane0 = pl.program_id(1) * DECAY_LANES
    mine = jnp.logical_and(lane >= head_lane0, lane < head_lane0 + DECAY_LANES)
    qam_ref[...] = jnp.where(mine, qa_ref[...], jnp.zeros(qa_ref.shape, BF16))
    ones = jnp.ones((TK, HEAD_DIM), BF16)

    def rows(blk, size):
        return slice(blk * size, (blk + 1) * size)

    def stage1(g):
        qi, kj = pairs[g]
        qr, kr = rows(qi, TQ), rows(kj, TK)
        qp = jnp.concatenate([q_ref[qr, :], qam_ref[qr, :]], axis=1)
        kp = jnp.concatenate([k_ref[kr, :], ka_ref[kr, :]], axis=1)
        s = lax.dot_general(qp, kp, (((1,), (1,)), ((), ())), preferred_element_type=F32)
        if qi == kj:
            r = lax.broadcasted_iota(jnp.int32, (TQ, TK), 0)
            c = lax.broadcasted_iota(jnp.int32, (TQ, TK), 1)
            s = jnp.where(c <= r, s, -jnp.inf)
        s_ref[g % ATTN_SLOTS] = s

    def stage2(g):
        qi, _ = pairs[g]
        qr, slot = rows(qi, TQ), g % ATTN_SLOTS
        bm = jnp.max(s_ref[slot], axis=1, keepdims=True)
        if first[g]:
            m_new = jnp.broadcast_to(bm, (TQ, LANES))
        else:
            m_prev = m_ref[qr, :]
            m_new = jnp.maximum(m_prev, bm)
            a_ref[slot] = jnp.exp2(m_prev - m_new)
        mn_ref[slot] = m_new
        m_ref[qr, :] = m_new

    def stage3(g):
        qi, kj = pairs[g]
        qr, kr, slot = rows(qi, TQ), rows(kj, TK), g % ATTN_SLOTS
        m_new = mn_ref[slot]
        p = jnp.exp2(s_ref[slot] - jnp.concatenate([m_new] * (TK // LANES), axis=1))
        vp = jnp.concatenate([v_ref[kr, :], ones], axis=1)
        pv = jnp.dot(p.astype(BF16), vp, preferred_element_type=F32)
        if first[g]:
            acc = pv
        else:
            alpha = a_ref[slot]
            acc = acc_ref[qr, :] * jnp.concatenate([alpha, alpha], axis=1) + pv
        if qi == kj:
            o_ref[qr, :] = (acc[:, :HEAD_DIM] / acc[:, HEAD_DIM:]).astype(BF16)
        else:
            acc_ref[qr, :] = acc

    seen = set()
    first = []
    for qi, _ in pairs:
        first.append(qi not in seen)
        seen.add(qi)
    total = len(pairs)
    for g in range(total + 2):
        if g >= 2:
            stage3(g - 2)
        if 1 <= g <= total:
            stage2(g - 1)
        if g < total:
            stage1(g)


def _attention(z, qa, ka, cast_weights, *, batch, seq, n_heads):
    assert TQ == TK and ATTN_SLOTS >= 3
    nq = seq // TQ
    t = batch * seq
    steps = batch * n_heads
    cast_specs = [pl.BlockSpec((w.shape[0] // steps, w.shape[1]),
                               lambda b, h: (b * n_heads + h, 0)) for w in cast_weights]
    pairs = tuple([(i, j) for i in range(nq) for j in range(i)] + [(i, i) for i in range(nq)])
    outs = pl.pallas_call(
        functools.partial(_attn_kernel, pairs=pairs, n_cast=len(cast_weights)),
        grid=(batch, n_heads),
        in_specs=[
            pl.BlockSpec((seq, HEAD_DIM), lambda b, h: (b, h)),
            pl.BlockSpec((seq, LANES), lambda b, h: (b, 0)),
            pl.BlockSpec((seq, HEAD_DIM), lambda b, h: (b, n_heads + h)),
            pl.BlockSpec((seq, LANES), lambda b, h: (b, 0)),
            pl.BlockSpec((seq, HEAD_DIM), lambda b, h: (b, 2 * n_heads + h)),
        ] + cast_specs,
        out_specs=[pl.BlockSpec((seq, HEAD_DIM), lambda b, h: (b, h))] + cast_specs,
        out_shape=[jax.ShapeDtypeStruct((t, n_heads * HEAD_DIM), BF16)]
        + [jax.ShapeDtypeStruct(w.shape, BF16) for w in cast_weights],
        scratch_shapes=[
            pltpu.VMEM((seq, LANES), BF16),
            pltpu.VMEM((seq, LANES), F32),
            pltpu.VMEM((seq, 2 * HEAD_DIM), F32),
            pltpu.VMEM((ATTN_SLOTS, TQ, TK), F32),
            pltpu.VMEM((ATTN_SLOTS, TQ, LANES), F32),
            pltpu.VMEM((ATTN_SLOTS, TQ, LANES), F32),
        ],
        compiler_params=_params("parallel", "parallel"),
        name="attention",
    )(z, qa, z, ka, z, *cast_weights)
    return outs[0], outs[1:]


def _mixout_kernel(zu_ref, zv_ref, a_ref, x_ref, lng_ref, lnb_ref, ws_ref, bs_ref, og_ref,
                   ag_ref, w_ref, o_ref, w_sc, v_sc, gm_sc, lhs0, lhs1):
    s = pl.program_id(0)
    tm = zu_ref.shape[0]
    n_heads = ws_ref.shape[0]
    d_attn = a_ref.shape[1]

    @pl.when(s == 0)
    def _():
        r = lax.broadcasted_iota(jnp.int32, (CHUNK, CHUNK), 0)
        c = lax.broadcasted_iota(jnp.int32, (CHUNK, CHUNK), 1)
        for h in range(n_heads):
            w_sc[h] = jnp.where(c <= r, ws_ref[h], 0.0).astype(BF16)
        lhs1[...] = jnp.zeros(lhs1.shape, BF16)

    def gate_values(cc, part):
        groups = _row_groups(CHUNK, zv_ref.shape[1], cc * CHUNK)
        for rows in groups[part * len(groups) // MIX_PARTS:(part + 1) * len(groups) // MIX_PARTS]:
            gv = _gelu(zv_ref[rows, :].astype(F32))
            mu = jnp.mean(gv, axis=-1, keepdims=True)
            xc = gv - mu
            var = jnp.mean(xc * xc, axis=-1, keepdims=True)
            v_sc[rows, :] = (xc * lax.rsqrt(var + EPS) * lng_ref[...]
                             + lnb_ref[...]).astype(BF16)

    def gate_mix(cc, part, lhs_w):
        rows = slice(cc * CHUNK, (cc + 1) * CHUNK)
        for h in range(part * n_heads // MIX_PARTS, (part + 1) * n_heads // MIX_PARTS):
            cols = slice(h * HEAD_DIM, (h + 1) * HEAD_DIM)
            mix = jnp.dot(w_sc[h], v_sc[rows, cols], preferred_element_type=F32) + bs_ref[h]
            gm_sc[rows, cols] = _gelu(zu_ref[rows, cols].astype(F32)) * mix
        if part == MIX_PARTS - 1:
            for sl in _row_groups(CHUNK, gm_sc.shape[1], cc * CHUNK):
                lhs_w[sl, d_attn:] = _rms(gm_sc[sl, :], og_ref[...]).astype(BF16)

    def attn_norm(cc, part, lhs_w):
        groups = _row_groups(CHUNK, d_attn, cc * CHUNK)
        for sl in groups[part * len(groups) // MIX_PARTS:(part + 1) * len(groups) // MIX_PARTS]:
            lhs_w[sl, :d_attn] = _rms(a_ref[sl, :].astype(F32), ag_ref[...]).astype(BF16)

    def step(lhs_w, lhs_r):
        n_slices = 2 * (tm // CHUNK)
        ncol = o_ref.shape[1] // n_slices
        kpart = w_ref.shape[0] // MIX_PARTS
        for k in range(n_slices):
            cols = slice(k * ncol, (k + 1) * ncol)
            for part in range(MIX_PARTS):
                ks = slice(part * kpart, (part + 1) * kpart)
                y = jnp.dot(lhs_r[:, ks], w_ref[ks, cols], preferred_element_type=F32)
                if part == 0:
                    o_ref[:, cols] = x_ref[:, cols] + y
                else:
                    o_ref[:, cols] += y
                if k % 2 == 0:
                    gate_values(k // 2, part)
                else:
                    gate_mix(k // 2, part, lhs_w)
                    attn_norm(k // 2, part, lhs_w)

    @pl.when(s % 2 == 0)
    def _():
        step(lhs0, lhs1)

    @pl.when(s % 2 == 1)
    def _():
        step(lhs1, lhs0)


def _mixout(z, attn, x2d, ln_g, ln_b, w_s, b_s3, gm_g, attn_g, w_out, *, d_attn, d_gmlp):
    t, d = x2d.shape
    n_heads = w_s.shape[0]
    ucol = 3 * d_attn // d_gmlp
    n = t // TM_OUT
    gate_blk = lambda s: jnp.minimum(s, n - 1)
    proj_blk = lambda s: jnp.maximum(s - 1, 0)
    return pl.pallas_call(
        _mixout_kernel,
        grid=(n + 1,),
        in_specs=[
            pl.BlockSpec((TM_OUT, d_gmlp), lambda s: (gate_blk(s), ucol)),
            pl.BlockSpec((TM_OUT, d_gmlp), lambda s: (gate_blk(s), ucol + 1)),
            pl.BlockSpec((TM_OUT, d_attn), lambda s: (gate_blk(s), 0)),
            pl.BlockSpec((TM_OUT, d), lambda s: (proj_blk(s), 0)),
            pl.BlockSpec((1, d_gmlp), lambda i: (0, 0)),
            pl.BlockSpec((1, d_gmlp), lambda i: (0, 0)),
            pl.BlockSpec((n_heads, CHUNK, CHUNK), lambda i: (0, 0, 0)),
            pl.BlockSpec((n_heads, CHUNK, 1), lambda i: (0, 0, 0)),
            pl.BlockSpec((1, d_gmlp), lambda i: (0, 0)),
            pl.BlockSpec((1, d_attn), lambda i: (0, 0)),
            pl.BlockSpec((d_attn + d_gmlp, d), lambda i: (0, 0)),
        ],
        out_specs=pl.BlockSpec((TM_OUT, d), lambda s: (proj_blk(s), 0)),
        out_shape=jax.ShapeDtypeStruct((t, d), F32),
        scratch_shapes=[
            pltpu.VMEM((n_heads, CHUNK, CHUNK), BF16),
            pltpu.VMEM((TM_OUT, d_gmlp), BF16),
            pltpu.VMEM((TM_OUT, d_gmlp), F32),
            pltpu.VMEM((TM_OUT, d_attn + d_gmlp), BF16),
            pltpu.VMEM((TM_OUT, d_attn + d_gmlp), BF16),
        ],
        compiler_params=_params("arbitrary"),
        name="mixout",
    )(z, z, attn, x2d, ln_g, ln_b, w_s, b_s3, gm_g, attn_g, w_out)


def _ffn_kernel(x_ref, g_ref, w1_ref, w2_ref, gf_ref, o_ref, h_ref, *, final_norm):
    f = pl.program_id(1)
    last = pl.num_programs(1) - 1
    tm = x_ref.shape[0]

    def hidden():
        a = jnp.maximum(jnp.dot(h_ref[...], w1_ref[...], preferred_element_type=F32), 0.0)
        return (a * a).astype(BF16)

    def mlp():
        return jnp.dot(hidden(), w2_ref[...], preferred_element_type=F32)

    @pl.when(f == 0)
    def _():
        for sl in _row_groups(tm, x_ref.shape[1]):
            h_ref[sl, :] = _rms(x_ref[sl, :], g_ref[...]).astype(BF16)
        o_ref[...] = mlp()

    @pl.when(jnp.logical_and(f > 0, f < last))
    def _():
        o_ref[...] += mlp()

    @pl.when(f == last)
    def _():
        a = hidden()
        part = tm // FFN_LAST_PARTS
        for r0 in range(0, tm, part):
            rows = slice(r0, r0 + part)
            y = x_ref[rows, :] + (o_ref[rows, :] + jnp.dot(
                a[rows, :], w2_ref[...], preferred_element_type=F32))
            if final_norm:
                for sl in _row_groups(part, y.shape[1]):
                    o_ref[r0 + sl.start:r0 + sl.stop, :] = _rms(y[sl, :], gf_ref[...])
            else:
                o_ref[rows, :] = y


def _ffn(x2d, g, w1, w2, g_final, *, final_norm):
    t, d = x2d.shape
    d_ff = w1.shape[1]
    assert d_ff // TF_FFN >= 2
    return pl.pallas_call(
        functools.partial(_ffn_kernel, final_norm=final_norm),
        grid=(t // TM_FFN, d_ff // TF_FFN),
        in_specs=[
            pl.BlockSpec((TM_FFN, d), lambda i, f: (i, 0)),
            pl.BlockSpec((1, d), lambda i, f: (0, 0)),
            pl.BlockSpec((d, TF_FFN), lambda i, f: (0, f)),
            pl.BlockSpec((TF_FFN, d), lambda i, f: (f, 0)),
            pl.BlockSpec((1, d), lambda i, f: (0, 0)),
        ],
        out_specs=pl.BlockSpec((TM_FFN, d), lambda i, f: (i, 0)),
        out_shape=jax.ShapeDtypeStruct((t, d), F32),
        scratch_shapes=[pltpu.VMEM((TM_FFN, d), BF16)],
        compiler_params=_params("parallel", "arbitrary"),
        name="ffn",
    )(x2d, g, w1, w2, g_final)


def kernel(x, norm_mix_g, w_in, b_f, gmlp_ln_g, gmlp_ln_b, w_s, b_s, attn_out_g, gmlp_out_g,
           w_out, norm_ffn_g, w_ff1, w_ff2, norm_final_g):
    batch, seq, d = x.shape
    depth = w_in.shape[0]
    n_heads = b_f.shape[1]
    d_attn = n_heads * HEAD_DIM
    d_gmlp = gmlp_ln_g.shape[1]
    o3 = 3 * d_attn
    o4 = o3 + n_heads
    assert d_attn == d_gmlp and w_in.shape[2] == o4 + 2 * d_gmlp

    x2d = x.reshape(batch * seq, d)
    for l in range(depth):
        w_main, w_gate = _wprep(jnp.swapaxes(w_in[l], 0, 1), o3=o3, n_gate=n_heads)
        w_f = jnp.swapaxes(_decay_lanes(jnp.swapaxes(w_gate, 0, 1), n_heads), 0, 1).astype(BF16)
        bf_row = _decay_lanes(b_f[l].astype(F32), n_heads).reshape(1, LANES)

        z, zf = _inproj(x2d, norm_mix_g[l].reshape(1, d), w_main, w_f, d_attn=d_attn)
        qa, ka = _decay(zf, bf_row, batch=batch, seq=seq, n_heads=n_heads)
        attn, (w_out16, w_ff1_16, w_ff2_16) = _attention(
            z, qa, ka, [w_out[l], w_ff1[l], w_ff2[l]], batch=batch, seq=seq, n_heads=n_heads)
        x2d = _mixout(z, attn, x2d, gmlp_ln_g[l].reshape(1, d_gmlp),
                      gmlp_ln_b[l].reshape(1, d_gmlp), w_s[l], b_s[l].reshape(n_heads, CHUNK, 1),
                      gmlp_out_g[l].reshape(1, d_gmlp), attn_out_g[l].reshape(1, d_attn),
                      w_out16, d_attn=d_attn, d_gmlp=d_gmlp)
        x2d = _ffn(x2d, norm_ffn_g[l].reshape(1, d), w_ff1_16, w_ff2_16,
                   norm_final_g.reshape(1, d), final_norm=(l == depth - 1))
    return x2d.reshape(batch, seq, d)
```

```python
import functools
import math

import jax
import jax.numpy as jnp
from jax import lax
from jax.experimental import pallas as pl
from jax.experimental.pallas import tpu as pltpu

F32 = jnp.float32
BF16 = jnp.bfloat16

HEAD_DIM = 128
CHUNK = 128
EPS = 1e-6
LANES = 128
LOG2E = 1.4426950408889634
VMEM_LIMIT = 62 * 1024 * 1024

TM_WPREP = 512
TM_PROJ = 1024
TN_PROJ = 2560
TM_DECAY = 2048
TQ = 512
TK = 512
ATTN_SLOTS = 4
TM_OUT = 512
MIX_PARTS = 2
TM_FFN = 1024
TF_FFN = 1024
FFN_LAST_PARTS = 4
NORM_VREGS = 32
CUM_BLK = 256
DECAY_LANES = 8


def _row_groups(n_rows, n_cols, start=0):
    step = max(16, NORM_VREGS * 8 * LANES // n_cols)
    assert n_rows % step == 0
    return [slice(start + r, start + r + step) for r in range(0, n_rows, step)]


def _rms(x, g):
    ms = jnp.mean(x * x, axis=-1, keepdims=True)
    return x * lax.rsqrt(ms + EPS) * g


def _rms_unit(x):
    return x * lax.rsqrt(jnp.mean(x * x, axis=-1, keepdims=True) + EPS)


def _gelu(x):
    c = math.sqrt(2.0 / math.pi)
    inner = x * (c + (c * 0.044715) * (x * x))
    hx = 0.5 * x
    return hx + hx * jnp.tanh(inner)


def _split3(x):
    hi = x.astype(BF16)
    r1 = x - hi.astype(F32)
    mid = r1.astype(BF16)
    lo = (r1 - mid.astype(F32)).astype(BF16)
    return hi, mid, lo


def _params(*sem):
    return pltpu.CompilerParams(dimension_semantics=sem, vmem_limit_bytes=VMEM_LIMIT)


def _wprep_kernel(a_ref, b_ref, main_ref, gate_ref, *, gate_block, n_gate):
    j = pl.program_id(0)

    @pl.when(j < gate_block)
    def _():
        main_ref[...] = a_ref[...].astype(BF16)

    @pl.when(j >= gate_block)
    def _():
        main_ref[...] = jnp.concatenate([a_ref[n_gate:, :], b_ref[...]], axis=0).astype(BF16)

    @pl.when(j == gate_block)
    def _():
        gate_ref[...] = a_ref[:n_gate, :]


def _wprep(wt, *, o3, n_gate):
    n, d = wt.shape
    assert o3 % TM_WPREP == 0 and (n - n_gate) % TM_WPREP == 0 and TM_WPREP % n_gate == 0
    assert n_gate % 8 == 0
    per = TM_WPREP // n_gate
    return pl.pallas_call(
        functools.partial(_wprep_kernel, gate_block=o3 // TM_WPREP, n_gate=n_gate),
        grid=((n - n_gate) // TM_WPREP,),
        in_specs=[
            pl.BlockSpec((TM_WPREP, d), lambda j: (j, 0)),
            pl.BlockSpec((n_gate, d), lambda j: ((j + 1) * per, 0)),
        ],
        out_specs=[
            pl.BlockSpec((TM_WPREP, d), lambda j: (j, 0)),
            pl.BlockSpec((n_gate, d), lambda j: (0, 0)),
        ],
        out_shape=[
            jax.ShapeDtypeStruct((n - n_gate, d), BF16),
            jax.ShapeDtypeStruct((n_gate, d), F32),
        ],
        compiler_params=_params("arbitrary"),
        name="wprep",
    )(wt, wt)


def _inproj_kernel(x_ref, g_ref, w_ref, wf_ref, z_ref, zf_ref, h_ref, *, d_q, q_scale):
    j = pl.program_id(1)
    tm = x_ref.shape[0]

    nt = (((1,), (1,)), ((), ()))

    @pl.when(j == 0)
    def _():
        for sl in _row_groups(tm, x_ref.shape[1]):
            h_ref[sl, :] = _rms(x_ref[sl, :], g_ref[...]).astype(BF16)
        zf_ref[...] = lax.dot_general(h_ref[...], wf_ref[...], nt, preferred_element_type=F32)
        q = lax.dot_general(h_ref[...], w_ref[:d_q, :], nt, preferred_element_type=F32)
        z_ref[:, :d_q] = (q * q_scale).astype(BF16)
        z_ref[:, d_q:] = lax.dot_general(h_ref[...], w_ref[d_q:, :], nt,
                                         preferred_element_type=F32).astype(BF16)

    @pl.when(j > 0)
    def _():
        z_ref[...] = lax.dot_general(h_ref[...], w_ref[...], nt,
                                     preferred_element_type=F32).astype(BF16)


def _inproj(x2d, g, w_main, w_f, *, d_attn):
    t, d = x2d.shape
    n = w_main.shape[0]
    q_scale = LOG2E / math.sqrt(HEAD_DIM)
    assert d_attn < TN_PROJ and d_attn % LANES == 0
    return pl.pallas_call(
        functools.partial(_inproj_kernel, d_q=d_attn, q_scale=q_scale),
        grid=(t // TM_PROJ, n // TN_PROJ),
        in_specs=[
            pl.BlockSpec((TM_PROJ, d), lambda i, j: (i, 0)),
            pl.BlockSpec((1, d), lambda i, j: (0, 0)),
            pl.BlockSpec((TN_PROJ, d), lambda i, j: (j, 0)),
            pl.BlockSpec((LANES, d), lambda i, j: (0, 0)),
        ],
        out_specs=[
            pl.BlockSpec((TM_PROJ, TN_PROJ), lambda i, j: (i, j)),
            pl.BlockSpec((TM_PROJ, LANES), lambda i, j: (i, 0)),
        ],
        out_shape=[
            jax.ShapeDtypeStruct((t, n), BF16),
            jax.ShapeDtypeStruct((t, LANES), F32),
        ],
        scratch_shapes=[pltpu.VMEM((TM_PROJ, d), BF16)],
        compiler_params=_params("parallel", "arbitrary"),
        name="inproj",
    )(x2d, g, w_main, w_f)


def _decay_kernel(zf_ref, bf_ref, qa_ref, ka_ref, f_ref, carry_ref, *, n_heads):
    rows = zf_ref.shape[0]

    @pl.when(pl.program_id(1) == 0)
    def _():
        carry_ref[...] = jnp.zeros(carry_ref.shape, F32)

    row = lax.broadcasted_iota(jnp.int32, (CUM_BLK, CUM_BLK), 0)
    col = lax.broadcasted_iota(jnp.int32, (CUM_BLK, CUM_BLK), 1)
    tri = (col <= row).astype(BF16)
    carry = carry_ref[...]
    for r in range(rows // CUM_BLK):
        sl = slice(r * CUM_BLK, (r + 1) * CUM_BLK)
        x = zf_ref[sl, :] + bf_ref[...]
        logf = jnp.minimum(x, 0.0) - jnp.log1p(jnp.exp(-jnp.abs(x)))
        hi, mid, lo = _split3(logf)
        c = (jnp.dot(tri, hi, preferred_element_type=F32)
             + jnp.dot(tri, mid, preferred_element_type=F32)
             + jnp.dot(tri, lo, preferred_element_type=F32)) + carry
        f_ref[sl, :] = c
        carry = c[CUM_BLK - 1:CUM_BLK, :]
    carry_ref[...] = carry

    hi, mid, lo = _split3(f_ref[...] * LOG2E)
    lane = lax.broadcasted_iota(jnp.int32, (rows, LANES), 1)
    sub = lane & (DECAY_LANES - 1)
    used = lane < n_heads * DECAY_LANES
    piece_idx = jnp.where(sub >= 3, sub - 3, sub)
    piece = jnp.where(piece_idx == 0, hi, jnp.where(piece_idx == 1, mid, lo)).astype(F32)
    first = jnp.logical_and(used, sub < 3)
    second = jnp.logical_and(used, jnp.logical_and(sub >= 3, sub < 6))
    qa_ref[...] = jnp.where(first, piece, jnp.where(second, 1.0, 0.0)).astype(BF16)
    ka_ref[...] = jnp.where(first, 1.0, jnp.where(second, -piece, 0.0)).astype(BF16)


def _decay_lanes(v, n_heads):
    lead = v.shape[:-1]
    rep = jnp.broadcast_to(v[..., None], lead + (n_heads, 6))
    rep = jnp.pad(rep, [(0, 0)] * len(lead) + [(0, 0), (0, DECAY_LANES - 6)])
    rep = rep.reshape(lead + (n_heads * DECAY_LANES,))
    return jnp.pad(rep, [(0, 0)] * len(lead) + [(0, LANES - n_heads * DECAY_LANES)])


def _decay(zf, bf_row, *, batch, seq, n_heads):
    assert n_heads * DECAY_LANES <= LANES
    ns = seq // TM_DECAY
    return pl.pallas_call(
        functools.partial(_decay_kernel, n_heads=n_heads),
        grid=(batch, ns),
        in_specs=[
            pl.BlockSpec((TM_DECAY, LANES), lambda b, s: (b * ns + s, 0)),
            pl.BlockSpec((1, LANES), lambda b, s: (0, 0)),
        ],
        out_specs=[
            pl.BlockSpec((TM_DECAY, LANES), lambda b, s: (b * ns + s, 0)),
            pl.BlockSpec((TM_DECAY, LANES), lambda b, s: (b * ns + s, 0)),
        ],
        out_shape=[
            jax.ShapeDtypeStruct((batch * seq, LANES), BF16),
            jax.ShapeDtypeStruct((batch * seq, LANES), BF16),
        ],
        scratch_shapes=[pltpu.VMEM((TM_DECAY, LANES), F32), pltpu.VMEM((1, LANES), F32)],
        compiler_params=_params("parallel", "arbitrary"),
        name="decay",
    )(zf, bf_row)


def _attn_kernel(q_ref, qa_ref, k_ref, ka_ref, v_ref, *rest, pairs, n_cast):
    cast_in, rest = rest[:n_cast], rest[n_cast:]
    gain_ref, rest = rest[0], rest[1:]
    o_ref, rest = rest[0], rest[1:]
    cast_out, rest = rest[:n_cast], rest[n_cast:]
    for idx, (src, dst) in enumerate(zip(cast_in, cast_out)):
        w = src[...]
        if idx == 0:
            w = w * gain_ref[...]
        dst[...] = w.astype(BF16)
    _attn_body(q_ref, qa_ref, k_ref, ka_ref, v_ref, o_ref, *rest, pairs=pairs)


def _attn_body(q_ref, qa_ref, k_ref, ka_ref, v_ref, o_ref,
               qam_ref, m_ref, acc_ref, s_ref, mn_ref, a_ref, *, pairs):
    lane = lax.broadcasted_iota(jnp.int32, qa_ref.shape, 1)
    head_lane0 = pl.program_id(1) * DECAY_LANES
    mine = jnp.logical_and(lane >= head_lane0, lane < head_lane0 + DECAY_LANES)
    qam_ref[...] = jnp.where(mine, qa_ref[...], jnp.zeros(qa_ref.shape, BF16))
    ones = jnp.ones((TK, HEAD_DIM), BF16)

    def rows(blk, size):
        return slice(blk * size, (blk + 1) * size)

    def stage1(g):
        qi, kj = pairs[g]
        qr, kr = rows(qi, TQ), rows(kj, TK)
        qp = jnp.concatenate([q_ref[qr, :], qam_ref[qr, :]], axis=1)
        kp = jnp.concatenate([k_ref[kr, :], ka_ref[kr, :]], axis=1)
        s = lax.dot_general(qp, kp, (((1,), (1,)), ((), ())), preferred_element_type=F32)
        if qi == kj:
            r = lax.broadcasted_iota(jnp.int32, (TQ, TK), 0)
            c = lax.broadcasted_iota(jnp.int32, (TQ, TK), 1)
            s = jnp.where(c <= r, s, -jnp.inf)
        s_ref[g % ATTN_SLOTS] = s

    def stage2(g):
        qi, _ = pairs[g]
        qr, slot = rows(qi, TQ), g % ATTN_SLOTS
        bm = jnp.max(s_ref[slot], axis=1, keepdims=True)
        if first[g]:
            m_new = jnp.broadcast_to(bm, (TQ, LANES))
        else:
            m_prev = m_ref[qr, :]
            m_new = jnp.maximum(m_prev, bm)
            a_ref[slot] = jnp.exp2(m_prev - m_new)
        mn_ref[slot] = m_new
        m_ref[qr, :] = m_new

    def stage3(g):
        qi, kj = pairs[g]
        qr, kr, slot = rows(qi, TQ), rows(kj, TK), g % ATTN_SLOTS
        m_new = mn_ref[slot]
        p = jnp.exp2(s_ref[slot] - jnp.concatenate([m_new] * (TK // LANES), axis=1))
        vp = jnp.concatenate([v_ref[kr, :], ones], axis=1)
        pv = jnp.dot(p.astype(BF16), vp, preferred_element_type=F32)
        if first[g]:
            acc = pv
        else:
            alpha = a_ref[slot]
            acc = acc_ref[qr, :] * jnp.concatenate([alpha, alpha], axis=1) + pv
        if qi == kj:
            o_ref[qr, :] = (acc[:, :HEAD_DIM] / acc[:, HEAD_DIM:]).astype(BF16)
        else:
            acc_ref[qr, :] = acc

    seen = set()
    first = []
    for qi, _ in pairs:
        first.append(qi not in seen)
        seen.add(qi)
    total = len(pairs)
    for g in range(total + 2):
        if g >= 2:
            stage3(g - 2)
        if 1 <= g <= total:
            stage2(g - 1)
        if g < total:
            stage1(g)


def _attention(z, qa, ka, cast_weights, row_gain, *, batch, seq, n_heads):
    assert TQ == TK and ATTN_SLOTS >= 3
    nq = seq // TQ
    t = batch * seq
    steps = batch * n_heads
    cast_specs = [pl.BlockSpec((w.shape[0] // steps, w.shape[1]),
                               lambda b, h: (b * n_heads + h, 0)) for w in cast_weights]
    pairs = tuple([(i, j) for i in range(nq) for j in range(i)] + [(i, i) for i in range(nq)])
    outs = pl.pallas_call(
        functools.partial(_attn_kernel, pairs=pairs, n_cast=len(cast_weights)),
        grid=(batch, n_heads),
        in_specs=[
            pl.BlockSpec((seq, HEAD_DIM), lambda b, h: (b, h)),
            pl.BlockSpec((seq, LANES), lambda b, h: (b, 0)),
            pl.BlockSpec((seq, HEAD_DIM), lambda b, h: (b, n_heads + h)),
            pl.BlockSpec((seq, LANES), lambda b, h: (b, 0)),
            pl.BlockSpec((seq, HEAD_DIM), lambda b, h: (b, 2 * n_heads + h)),
        ] + cast_specs + [pl.BlockSpec((row_gain.shape[0] // steps, 1),
                                       lambda b, h: (b * n_heads + h, 0))],
        out_specs=[pl.BlockSpec((seq, HEAD_DIM), lambda b, h: (b, h))] + cast_specs,
        out_shape=[jax.ShapeDtypeStruct((t, n_heads * HEAD_DIM), BF16)]
        + [jax.ShapeDtypeStruct(w.shape, BF16) for w in cast_weights],
        scratch_shapes=[
            pltpu.VMEM((seq, LANES), BF16),
            pltpu.VMEM((seq, LANES), F32),
            pltpu.VMEM((seq, 2 * HEAD_DIM), F32),
            pltpu.VMEM((ATTN_SLOTS, TQ, TK), F32),
            pltpu.VMEM((ATTN_SLOTS, TQ, LANES), F32),
            pltpu.VMEM((ATTN_SLOTS, TQ, LANES), F32),
        ],
        compiler_params=_params("parallel", "parallel"),
        name="attention",
    )(z, qa, z, ka, z, *cast_weights, row_gain)
    return outs[0], outs[1:]


def _mixout_kernel(zu_ref, zv_ref, a_ref, x_ref, lng_ref, lnb_ref, ws_ref, bs_ref,
                   w_ref, o_ref, w_sc, v_sc, gm_sc, gmn0, gmn1, an_sc):
    s = pl.program_id(0)
    tm = zu_ref.shape[0]
    n_heads = ws_ref.shape[0]
    d_attn = a_ref.shape[1]

    @pl.when(s == 0)
    def _():
        r = lax.broadcasted_iota(jnp.int32, (CHUNK, CHUNK), 0)
        c = lax.broadcasted_iota(jnp.int32, (CHUNK, CHUNK), 1)
        for h in range(n_heads):
            w_sc[h] = jnp.where(c <= r, ws_ref[h], 0.0).astype(BF16)
        gmn1[...] = jnp.zeros(gmn1.shape, BF16)

    def gate_values(cc, part):
        groups = _row_groups(CHUNK, zv_ref.shape[1], cc * CHUNK)
        for rows in groups[part * len(groups) // MIX_PARTS:(part + 1) * len(groups) // MIX_PARTS]:
            gv = _gelu(zv_ref[rows, :].astype(F32))
            mu = jnp.mean(gv, axis=-1, keepdims=True)
            xc = gv - mu
            var = jnp.mean(xc * xc, axis=-1, keepdims=True)
            v_sc[rows, :] = (xc * lax.rsqrt(var + EPS) * lng_ref[...]
                             + lnb_ref[...]).astype(BF16)

    def gate_mix(cc, part, gmn_w):
        rows = slice(cc * CHUNK, (cc + 1) * CHUNK)
        for h in range(part * n_heads // MIX_PARTS, (part + 1) * n_heads // MIX_PARTS):
            cols = slice(h * HEAD_DIM, (h + 1) * HEAD_DIM)
            mix = jnp.dot(w_sc[h], v_sc[rows, cols], preferred_element_type=F32) + bs_ref[h]
            gm_sc[rows, cols] = _gelu(zu_ref[rows, cols].astype(F32)) * mix
        if part == MIX_PARTS - 1:
            for sl in _row_groups(CHUNK, gm_sc.shape[1], cc * CHUNK):
                gmn_w[sl, :] = _rms_unit(gm_sc[sl, :]).astype(BF16)

    def step(gmn_w, gmn_r):
        for sl in _row_groups(tm, d_attn):
            an_sc[sl, :] = _rms_unit(a_ref[sl, :].astype(F32)).astype(BF16)
        n_slices = 2 * (tm // CHUNK)
        ncol = o_ref.shape[1] // n_slices
        kpart = w_ref.shape[0] // MIX_PARTS
        lhs = jnp.concatenate([an_sc[...], gmn_r[...]], axis=1)
        for k in range(n_slices):
            cols = slice(k * ncol, (k + 1) * ncol)
            for part in range(MIX_PARTS):
                ks = slice(part * kpart, (part + 1) * kpart)
                y = jnp.dot(lhs[:, ks], w_ref[ks, cols], preferred_element_type=F32)
                if part == 0:
                    o_ref[:, cols] = x_ref[:, cols] + y
                else:
                    o_ref[:, cols] += y
                if k % 2 == 0:
                    gate_values(k // 2, part)
                else:
                    gate_mix(k // 2, part, gmn_w)

    @pl.when(s % 2 == 0)
    def _():
        step(gmn0, gmn1)

    @pl.when(s % 2 == 1)
    def _():
        step(gmn1, gmn0)


def _mixout(z, attn, x2d, ln_g, ln_b, w_s, b_s3, w_out, *, d_attn, d_gmlp):
    t, d = x2d.shape
    n_heads = w_s.shape[0]
    ucol = 3 * d_attn // d_gmlp
    n = t // TM_OUT
    gate_blk = lambda s: jnp.minimum(s, n - 1)
    proj_blk = lambda s: jnp.maximum(s - 1, 0)
    return pl.pallas_call(
        _mixout_kernel,
        grid=(n + 1,),
        in_specs=[
            pl.BlockSpec((TM_OUT, d_gmlp), lambda s: (gate_blk(s), ucol)),
            pl.BlockSpec((TM_OUT, d_gmlp), lambda s: (gate_blk(s), ucol + 1)),
            pl.BlockSpec((TM_OUT, d_attn), lambda s: (proj_blk(s), 0)),
            pl.BlockSpec((TM_OUT, d), lambda s: (proj_blk(s), 0)),
            pl.BlockSpec((1, d_gmlp), lambda i: (0, 0)),
            pl.BlockSpec((1, d_gmlp), lambda i: (0, 0)),
            pl.BlockSpec((n_heads, CHUNK, CHUNK), lambda i: (0, 0, 0)),
            pl.BlockSpec((n_heads, CHUNK, 1), lambda i: (0, 0, 0)),
            pl.BlockSpec((d_attn + d_gmlp, d), lambda i: (0, 0)),
        ],
        out_specs=pl.BlockSpec((TM_OUT, d), lambda s: (proj_blk(s), 0)),
        out_shape=jax.ShapeDtypeStruct((t, d), F32),
        scratch_shapes=[
            pltpu.VMEM((n_heads, CHUNK, CHUNK), BF16),
            pltpu.VMEM((TM_OUT, d_gmlp), BF16),
            pltpu.VMEM((TM_OUT, d_gmlp), F32),
            pltpu.VMEM((TM_OUT, d_gmlp), BF16),
            pltpu.VMEM((TM_OUT, d_gmlp), BF16),
            pltpu.VMEM((TM_OUT, d_attn), BF16),
        ],
        compiler_params=_params("arbitrary"),
        name="mixout",
    )(z, z, attn, x2d, ln_g, ln_b, w_s, b_s3, w_out)


def _ffn_kernel(x_ref, g_ref, w1_ref, w2_ref, gf_ref, o_ref, h_ref, *, final_norm):
    f = pl.program_id(1)
    last = pl.num_programs(1) - 1
    tm = x_ref.shape[0]

    def hidden():
        a = jnp.maximum(jnp.dot(h_ref[...], w1_ref[...], preferred_element_type=F32), 0.0)
        return (a * a).astype(BF16)

    def mlp():
        return jnp.dot(hidden(), w2_ref[...], preferred_element_type=F32)

    @pl.when(f == 0)
    def _():
        for sl in _row_groups(tm, x_ref.shape[1]):
            h_ref[sl, :] = _rms(x_ref[sl, :], g_ref[...]).astype(BF16)
        o_ref[...] = mlp()

    @pl.when(jnp.logical_and(f > 0, f < last))
    def _():
        o_ref[...] += mlp()

    @pl.when(f == last)
    def _():
        a = hidden()
        part = tm // FFN_LAST_PARTS
        for r0 in range(0, tm, part):
            rows = slice(r0, r0 + part)
            y = x_ref[rows, :] + (o_ref[rows, :] + jnp.dot(
                a[rows, :], w2_ref[...], preferred_element_type=F32))
            if final_norm:
                for sl in _row_groups(part, y.shape[1]):
                    o_ref[r0 + sl.start:r0 + sl.stop, :] = _rms(y[sl, :], gf_ref[...])
            else:
                o_ref[rows, :] = y


def _ffn(x2d, g, w1, w2, g_final, *, final_norm):
    t, d = x2d.shape
    d_ff = w1.shape[1]
    assert d_ff // TF_FFN >= 2
    return pl.pallas_call(
        functools.partial(_ffn_kernel, final_norm=final_norm),
        grid=(t // TM_FFN, d_ff // TF_FFN),
        in_specs=[
            pl.BlockSpec((TM_FFN, d), lambda i, f: (i, 0)),
            pl.BlockSpec((1, d), lambda i, f: (0, 0)),
            pl.BlockSpec((d, TF_FFN), lambda i, f: (0, f)),
            pl.BlockSpec((TF_FFN, d), lambda i, f: (f, 0)),
            pl.BlockSpec((1, d), lambda i, f: (0, 0)),
        ],
        out_specs=pl.BlockSpec((TM_FFN, d), lambda i, f: (i, 0)),
        out_shape=jax.ShapeDtypeStruct((t, d), F32),
        scratch_shapes=[pltpu.VMEM((TM_FFN, d), BF16)],
        compiler_params=_params("parallel", "arbitrary"),
        name="ffn",
    )(x2d, g, w1, w2, g_final)


def kernel(x, norm_mix_g, w_in, b_f, gmlp_ln_g, gmlp_ln_b, w_s, b_s, attn_out_g, gmlp_out_g,
           w_out, norm_ffn_g, w_ff1, w_ff2, norm_final_g):
    batch, seq, d = x.shape
    depth = w_in.shape[0]
    n_heads = b_f.shape[1]
    d_attn = n_heads * HEAD_DIM
    d_gmlp = gmlp_ln_g.shape[1]
    o3 = 3 * d_attn
    o4 = o3 + n_heads
    assert d_attn == d_gmlp and w_in.shape[2] == o4 + 2 * d_gmlp

    x2d = x.reshape(batch * seq, d)
    for l in range(depth):
        w_main, w_gate = _wprep(jnp.swapaxes(w_in[l], 0, 1), o3=o3, n_gate=n_heads)
        w_f = jnp.swapaxes(_decay_lanes(jnp.swapaxes(w_gate, 0, 1), n_heads), 0, 1).astype(BF16)
        bf_row = _decay_lanes(b_f[l].astype(F32), n_heads).reshape(1, LANES)

        z, zf = _inproj(x2d, norm_mix_g[l].reshape(1, d), w_main, w_f, d_attn=d_attn)
        qa, ka = _decay(zf, bf_row, batch=batch, seq=seq, n_heads=n_heads)
        out_gain = jnp.concatenate([attn_out_g[l], gmlp_out_g[l]]).astype(F32).reshape(-1, 1)
        attn, (w_out16, w_ff1_16, w_ff2_16) = _attention(
            z, qa, ka, [w_out[l], w_ff1[l], w_ff2[l]], out_gain,
            batch=batch, seq=seq, n_heads=n_heads)
        x2d = _mixout(z, attn, x2d, gmlp_ln_g[l].reshape(1, d_gmlp),
                      gmlp_ln_b[l].reshape(1, d_gmlp), w_s[l], b_s[l].reshape(n_heads, CHUNK, 1),
                      w_out16, d_attn=d_attn, d_gmlp=d_gmlp)
        x2d = _ffn(x2d, norm_ffn_g[l].reshape(1, d), w_ff1_16, w_ff2_16,
                   norm_final_g.reshape(1, d), final_norm=(l == depth - 1))
    return x2d.reshape(batch, seq, d)
```
